```python
import math
import jax, jax.numpy as jnp
from jax import lax
import numpy as np

D_MODEL = 1024
BATCH = 32
SEQ = 2048
DEPTH = 4

CHUNK = 64
N_META = 16
NORM_EPS = 1e-6

GDN_HEADS = 8
GDN_HEAD_DIM = D_MODEL // GDN_HEADS
GDN_WIDTH = GDN_HEADS * GDN_HEAD_DIM
GDN_CONV = 4
CONF_WIDTH = D_MODEL // 2
CONF_KERNEL = 31
SC_WIDTH = D_MODEL // 2
SC_KERNEL = 3

D_MIX = GDN_WIDTH + CONF_WIDTH + SC_WIDTH
IN_SIZES = (3 * GDN_WIDTH, GDN_WIDTH, GDN_HEADS, GDN_HEADS,
            CONF_WIDTH, CONF_WIDTH, CONF_WIDTH,
            SC_WIDTH, SC_WIDTH, SC_WIDTH, SC_WIDTH)
N_IN = sum(IN_SIZES)

kernel_name = "hymba_gdn_conformer_shortconv_trunk"


def rms_norm(x, g):
    xf = x.astype(jnp.float32)
    y = xf * lax.rsqrt(jnp.mean(xf * xf, axis=-1, keepdims=True) + NORM_EPS)
    return (y * g.astype(jnp.float32)).astype(x.dtype)


def layer_norm(x, g, b):
    xf = x.astype(jnp.float32)
    mu = jnp.mean(xf, axis=-1, keepdims=True)
    xc = xf - mu
    y = xc * lax.rsqrt(jnp.mean(xc * xc, axis=-1, keepdims=True) + NORM_EPS)
    return (y * g.astype(jnp.float32) + b.astype(jnp.float32)).astype(x.dtype)


def l2_norm(x):
    return x * lax.rsqrt(jnp.sum(x * x, axis=-1, keepdims=True) + NORM_EPS)


def causal_dwconv(x, w):
    k_w = w.shape[0]
    return lax.conv_general_dilated(
        x, w.astype(x.dtype)[:, None, :], window_strides=(1,), padding=[(k_w - 1, 0)],
        dimension_numbers=('NWC', 'WIO', 'NWC'), feature_group_count=x.shape[-1])


def split_cols(p, sizes):
    idx = []
    acc = 0
    for s in sizes[:-1]:
        acc += s
        idx.append(acc)
    return jnp.split(p, idx, axis=-1)


def gated_delta_rule(q, k, v, beta, g):
    bsz, seq_len, n_h, d_k = q.shape
    d_v = v.shape[-1]
    pad = (-seq_len) % CHUNK
    pad4 = ((0, 0), (pad, 0), (0, 0), (0, 0))
    pad3 = ((0, 0), (pad, 0), (0, 0))
    q, k, v = jnp.pad(q, pad4), jnp.pad(k, pad4), jnp.pad(v, pad4)
    beta, g = jnp.pad(beta, pad3), jnp.pad(g, pad3)
    lp = seq_len + pad
    n_c = lp // CHUNK

    def blk4(t):
        return t.reshape(bsz, n_c, CHUNK, n_h, t.shape[-1]).transpose(0, 3, 1, 2, 4)

    def blk3(t):
        return t.reshape(bsz, n_c, CHUNK, n_h).transpose(0, 3, 1, 2)

    q, k, v = blk4(q), blk4(k), blk4(v)
    beta, g = blk3(beta), blk3(g)
    gc = jnp.cumsum(g, axis=-1)
    k_beta = k * beta[..., None]
    v_beta = v * beta[..., None]

    incl = jnp.tril(jnp.ones((CHUNK, CHUNK), dtype=bool))
    strict = jnp.tril(jnp.ones((CHUNK, CHUNK), dtype=bool), -1)
    decay = jnp.exp(jnp.where(incl, gc[..., :, None] - gc[..., None, :], -jnp.inf))

    kk = jnp.einsum('bhncd,bhnmd->bhncm', k_beta, k) * decay
    eye = jnp.eye(CHUNK, dtype=q.dtype)
    a_mat = eye + jnp.where(strict, kk, 0.0)
    rhs = jnp.concatenate([v_beta, k_beta * jnp.exp(gc)[..., None]], axis=-1)
    sol = lax.linalg.triangular_solve(a_mat, rhs, left_side=True, lower=True, unit_diagonal=True)
    u_val = sol[..., :d_v]
    w_kd = sol[..., d_v:]

    qk = jnp.einsum('bhncd,bhnmd->bhncm', q, k) * decay
    q_dec = q * jnp.exp(gc)[..., None]
    k_dec = k * jnp.exp(gc[..., -1:] - gc)[..., None]
    g_last = jnp.exp(gc[..., -1])

    def to_front(t):
        return jnp.moveaxis(t, 2, 0)

    xs = (to_front(u_val), to_front(w_kd), to_front(qk), to_front(q_dec), to_front(k_dec),
          to_front(g_last))

    def step(state, inp):
        u_n, w_n, qk_n, qd_n, kd_n, gl_n = inp
        v_new = u_n - jnp.einsum('bhcd,bhde->bhce', w_n, state)
        o_n = (jnp.einsum('bhcd,bhde->bhce', qd_n, state)
               + jnp.einsum('bhcm,bhme->bhce', qk_n, v_new))
        state = state * gl_n[..., None, None] + jnp.einsum('bhcd,bhce->bhde', kd_n, v_new)
        return state, o_n

    s0 = jnp.zeros((bsz, n_h, d_k, d_v), dtype=q.dtype)
    _, o = lax.scan(step, s0, xs)
    o = o.transpose(1, 0, 3, 2, 4).reshape(bsz, lp, n_h, d_v)
    return o[:, pad:]


def hybrid_layer(x, norm_g, w_in, conv_qkv, a_log, dt_bias, gdn_norm_g,
                 conf_dw_w, conf_dw_b, conf_ln_g, conf_ln_b, conf_pw_w, conf_pw_b,
                 sc_conv_w, w_out):
    bsz, seq_len, _ = x.shape
    f32 = jnp.float32
    h = rms_norm(x, norm_g)
    p = h @ w_in.astype(h.dtype)
    (qkv, z_a, b_lin, a_lin, u, u_gate, z_b, c_b, c_c, c_h, z_c) = split_cols(p, IN_SIZES)

    qkv = jax.nn.silu(causal_dwconv(qkv, conv_qkv)).astype(f32)
    q, k, v = jnp.split(qkv, 3, axis=-1)
    q = l2_norm(q.reshape(bsz, seq_len, GDN_HEADS, GDN_HEAD_DIM)) * (GDN_HEAD_DIM ** -0.5)
    k = l2_norm(k.reshape(bsz, seq_len, GDN_HEADS, GDN_HEAD_DIM))
    v = v.reshape(bsz, seq_len, GDN_HEADS, GDN_HEAD_DIM)
    beta = jax.nn.sigmoid(b_lin.astype(f32))
    g = -jnp.exp(a_log.astype(f32)) * jax.nn.softplus(a_lin.astype(f32) + dt_bias.astype(f32))
    o = gated_delta_rule(q, k, v, beta, g)
    o = o * lax.rsqrt(jnp.mean(o * o, axis=-1, keepdims=True) + NORM_EPS)
    o = o * gdn_norm_g.astype(f32) * jax.nn.silu(
        z_a.astype(f32).reshape(bsz, seq_len, GDN_HEADS, GDN_HEAD_DIM))
    out_a = o.reshape(bsz, seq_len, GDN_WIDTH).astype(x.dtype)

    y = u * jax.nn.sigmoid(u_gate)
    y = causal_dwconv(y, conf_dw_w) + conf_dw_b.astype(y.dtype)
    y = jax.nn.silu(layer_norm(y, conf_ln_g, conf_ln_b))
    y = y @ conf_pw_w.astype(y.dtype) + conf_pw_b.astype(y.dtype)
    out_b = y * jax.nn.silu(z_b)

    y = c_b * causal_dwconv(c_c * c_h, sc_conv_w)
    out_c = y * jax.nn.silu(z_c)

    mix = jnp.concatenate([out_a, out_b, out_c], axis=-1)
    return x + mix @ w_out.astype(mix.dtype)


def setup_inputs(seed: int = 0) -> dict:
    key = jax.random.key(seed)
    ks = jax.random.split(key, 17)
    f32 = jnp.float32

    def nrm(k, shape, scale):
        return jax.random.normal(k, shape, f32) * scale

    x = nrm(ks[0], (BATCH, SEQ, D_MODEL), 1.0)
    meta_tokens = nrm(ks[1], (N_META, D_MODEL), 1.0)
    norm_g = 1.0 + nrm(ks[2], (DEPTH, D_MODEL), 0.02)
    w_in = nrm(ks[3], (DEPTH, D_MODEL, N_IN), D_MODEL ** -0.5)
    conv_qkv = nrm(ks[4], (DEPTH, GDN_CONV, 3 * GDN_WIDTH), GDN_CONV ** -0.5)
    a_log = jnp.log(jax.random.uniform(ks[5], (DEPTH, GDN_HEADS), f32, 1.0, 16.0))
    dt = jnp.exp(jax.random.uniform(ks[6], (DEPTH, GDN_HEADS), f32,
                                    math.log(1e-3), math.log(1e-1)))
    dt_bias = dt + jnp.log(-jnp.expm1(-dt))
    gdn_norm_g = 1.0 + nrm(ks[7], (DEPTH, GDN_HEAD_DIM), 0.02)
    conf_dw_w = nrm(ks[8], (DEPTH, CONF_KERNEL, CONF_WIDTH), CONF_KERNEL ** -0.5)
    conf_dw_b = nrm(ks[9], (DEPTH, CONF_WIDTH), 0.02)
    conf_ln_g = 1.0 + nrm(ks[10], (DEPTH, CONF_WIDTH), 0.02)
    conf_ln_b = nrm(ks[11], (DEPTH, CONF_WIDTH), 0.02)
    conf_pw_w = nrm(ks[12], (DEPTH, CONF_WIDTH, CONF_WIDTH), CONF_WIDTH ** -0.5)
    conf_pw_b = nrm(ks[13], (DEPTH, CONF_WIDTH), 0.02)
    sc_conv_w = nrm(ks[14], (DEPTH, SC_KERNEL, SC_WIDTH), SC_KERNEL ** -0.5)
    w_out = nrm(ks[15], (DEPTH, D_MIX, D_MODEL), D_MIX ** -0.5)
    final_norm_g = 1.0 + nrm(ks[16], (D_MODEL,), 0.02)
    return {"x": x, "meta_tokens": meta_tokens, "norm_g": norm_g, "w_in": w_in,
            "conv_qkv": conv_qkv, "a_log": a_log, "dt_bias": dt_bias,
            "gdn_norm_g": gdn_norm_g, "conf_dw_w": conf_dw_w, "conf_dw_b": conf_dw_b,
            "conf_ln_g": conf_ln_g, "conf_ln_b": conf_ln_b, "conf_pw_w": conf_pw_w,
            "conf_pw_b": conf_pw_b, "sc_conv_w": sc_conv_w, "w_out": w_out,
            "final_norm_g": final_norm_g}


def reference(x, meta_tokens, norm_g, w_in, conv_qkv, a_log, dt_bias, gdn_norm_g,
              conf_dw_w, conf_dw_b, conf_ln_g, conf_ln_b, conf_pw_w, conf_pw_b,
              sc_conv_w, w_out, final_norm_g):
    bsz = x.shape[0]
    meta = jnp.broadcast_to(meta_tokens.astype(x.dtype)[None], (bsz, N_META, D_MODEL))
    h = jnp.concatenate([meta, x], axis=1)
    for l in range(DEPTH):
        h = hybrid_layer(h, norm_g[l], w_in[l], conv_qkv[l], a_log[l], dt_bias[l],
                         gdn_norm_g[l], conf_dw_w[l], conf_dw_b[l], conf_ln_g[l],
                         conf_ln_b[l], conf_pw_w[l], conf_pw_b[l], sc_conv_w[l], w_out[l])
    h = rms_norm(h, final_norm_g)
    return h[:, N_META:]
```

```python
import functools

import jax
import jax.numpy as jnp
from jax import lax
from jax.experimental import pallas as pl
from jax.experimental.pallas import tpu as pltpu

F32 = jnp.float32
BF16 = jnp.bfloat16

D_MODEL = 1024
N_META = 16
CHUNK = 64
NORM_EPS = 1e-6
GDN_HEADS = 8
GDN_HEAD_DIM = 128
GDN_WIDTH = GDN_HEADS * GDN_HEAD_DIM
GDN_CONV = 4
CONF_WIDTH = 512
CONF_KERNEL = 31
SC_WIDTH = 512
SC_KERNEL = 3
D_MIX = GDN_WIDTH + CONF_WIDTH + SC_WIDTH

SUBLANES = 8
LANES = 128
QKV_HIST = 8
CONF_HIST = 32
VMEM_LIMIT_BYTES = 48 * 1024 * 1024


def _sigmoid(x):
    return 1.0 / (1.0 + jnp.exp(-x))


def _silu(x):
    return x * _sigmoid(x)


def _dot(a, b):
    return jnp.dot(a, b, preferred_element_type=F32)


def _dot_exact(a, b, dims):
    return lax.dot_general(a, b, (dims, ((), ())), precision=lax.Precision.HIGHEST,
                           preferred_element_type=F32)


def _rms_rows(x_ref, g_ref, h_ref, n_blocks):
    def body(rb, carry):
        r0 = pl.multiple_of(rb * CHUNK, CHUNK)
        x = x_ref[pl.ds(r0, CHUNK), :]
        ms = jnp.mean(x * x, axis=-1, keepdims=True)
        h_ref[pl.ds(r0, CHUNK), :] = (x * lax.rsqrt(ms + NORM_EPS) * g_ref[...]).astype(BF16)
        return carry
    lax.fori_loop(0, n_blocks, body, 0)


def _gdn_kernel(x_ref, ng_ref, wa_ref, wba_ref, cw_ref, alog_ref, dtb_ref, gng_ref,
                out_ref, h_s, qkv_buf, q_s, k_s, v_s, za_s, beta_s, g_s, st_ref, *, tl, pad):
    t = pl.program_id(1)
    n_blocks = tl // CHUNK

    @pl.when(t == 0)
    def _():
        qkv_buf[0:QKV_HIST, :] = jnp.zeros((QKV_HIST, 3 * GDN_WIDTH), F32)
        st_ref[...] = jnp.zeros_like(st_ref)

    @pl.when(t > 0)
    def _():
        qkv_buf[0:QKV_HIST, :] = qkv_buf[tl:tl + QKV_HIST, :]

    _rms_rows(x_ref, ng_ref, h_s, n_blocks)

    h = h_s[...]
    for n0 in range(0, 3 * GDN_WIDTH, 512):
        qkv_buf[QKV_HIST:QKV_HIST + tl, n0:n0 + 512] = _dot(h, wa_ref[:, n0:n0 + 512])
    for n0 in range(0, GDN_WIDTH, 512):
        za_s[:, n0:n0 + 512] = _dot(h, wa_ref[:, 3 * GDN_WIDTH + n0:3 * GDN_WIDTH + n0 + 512])
    ba = _dot(h, wba_ref[...])
    row = lax.broadcasted_iota(jnp.int32, (tl, LANES), 0) + t * tl
    live = row >= pad
    beta_s[...] = jnp.where(live, _sigmoid(ba[:, 0:LANES]), 0.0)
    a_in = ba[:, LANES:2 * LANES] + dtb_ref[...]
    softplus = jnp.maximum(a_in, 0.0) + jnp.log1p(jnp.exp(-jnp.abs(a_in)))
    g_s[...] = jnp.where(live, -jnp.exp(alog_ref[...]) * softplus, 0.0)

    def conv_body(rb, carry):
        r0 = pl.multiple_of(rb * CHUNK, CHUNK)
        for cb in range(3 * GDN_HEADS):
            cs = slice(cb * LANES, (cb + 1) * LANES)
            win = qkv_buf[pl.ds(r0, CHUNK + QKV_HIST), cs]
            acc = None
            for j in range(GDN_CONV):
                o = QKV_HIST - (GDN_CONV - 1) + j
                term = win[o:o + CHUNK] * cw_ref[j:j + 1, cs]
                acc = term if acc is None else acc + term
            y = _silu(acc)
            if cb < 2 * GDN_HEADS:
                y = y * lax.rsqrt(jnp.sum(y * y, axis=-1, keepdims=True) + NORM_EPS)
            if cb < GDN_HEADS:
                y = y * (GDN_HEAD_DIM ** -0.5)
            dst = (q_s, k_s, v_s)[cb // GDN_HEADS]
            hc = cb % GDN_HEADS
            dst[pl.ds(r0, CHUNK), hc * LANES:(hc + 1) * LANES] = y
        return carry
    lax.fori_loop(0, n_blocks, conv_body, 0)

    ri = lax.broadcasted_iota(jnp.int32, (CHUNK, CHUNK), 0)
    ci = lax.broadcasted_iota(jnp.int32, (CHUNK, CHUNK), 1)
    incl = ri >= ci
    strict = ri > ci
    tri_lo = incl.astype(F32)
    tri_up = (ri <= ci).astype(F32)
    eye = (ri == ci).astype(F32)

    def chunk_body(c, carry):
        r0 = pl.multiple_of(c * CHUNK, CHUNK)
        rows = pl.ds(r0, CHUNK)
        g_c = g_s[rows, :]
        beta_c = beta_s[rows, :]
        gc = _dot_exact(tri_lo, g_c, ((1,), (0,)))
        gct = _dot_exact(g_c, tri_up, ((0,), (0,)))
        for hd in range(GDN_HEADS):
            cs = slice(hd * LANES, (hd + 1) * LANES)
            q = q_s[rows, cs]
            k = k_s[rows, cs]
            v = v_s[rows, cs]
            gcol = gc[:, hd:hd + 1]
            grow = gct[hd:hd + 1, :]
            glast = gc[CHUNK - 1:CHUNK, hd:hd + 1]
            bcol = beta_c[:, hd:hd + 1]
            eg = jnp.exp(gcol)
            kb = k * bcol
            vb = v * bcol
            kbg = kb * eg
            qd = q * eg
            kdec = k * jnp.exp(glast - gcol)
            k16 = k.astype(BF16)
            qkkk = lax.dot_general(jnp.concatenate([q, kb], axis=0).astype(BF16), k16,
                                   (((1,), (1,)), ((), ())), preferred_element_type=F32)
            dec = jnp.where(incl, jnp.exp(gcol - grow), 0.0)
            qk = qkkk[0:CHUNK] * dec
            nmat = jnp.where(strict, qkkk[CHUNK:2 * CHUNK] * dec, 0.0)
            tinv = eye - nmat
            xp = nmat
            for _ in range(5):
                x16 = xp.astype(BF16)
                xp = _dot(x16, x16)
                tinv = tinv + _dot(tinv.astype(BF16), xp.astype(BF16))
            uw = _dot(tinv.astype(BF16), jnp.concatenate([vb, kbg], axis=1).astype(BF16))
            u = uw[:, 0:LANES]
            w = uw[:, LANES:2 * LANES]
            s_old = st_ref[hd]
            wq = _dot(jnp.concatenate([w, qd], axis=0).astype(BF16), s_old.astype(BF16))
            v_new = u - wq[0:CHUNK]
            vn16 = v_new.astype(BF16)
            o = wq[CHUNK:2 * CHUNK] + _dot(qk.astype(BF16), vn16)
            st_ref[hd] = s_old * jnp.exp(glast) + lax.dot_general(
                kdec.astype(BF16), vn16, (((0,), (0,)), ((), ())), preferred_element_type=F32)
            o = o * lax.rsqrt(jnp.mean(o * o, axis=-1, keepdims=True) + NORM_EPS)
            o = o * gng_ref[...] * _silu(za_s[rows, cs])
            out_ref[rows, cs] = o.astype(BF16)
        return carry
    lax.fori_loop(0, n_blocks, chunk_body, 0)


def _conv_kernel(x_ref, ng_ref, wbc_ref, dww_ref, dwb_ref, lng_ref, lnb_ref, pww_ref, pwb_ref,
                 scw_ref, out_ref, h_s, p_s, y_buf, cc_buf, yn_s, *, tl):
    t = pl.program_id(1)
    n_blocks = tl // CHUNK
    W = CONF_WIDTH
    U, UG, ZB, CB, CC, CH, ZC = (slice(i * W, (i + 1) * W) for i in range(7))

    @pl.when(t == 0)
    def _():
        y_buf[0:CONF_HIST, :] = jnp.zeros((CONF_HIST, W), F32)
        cc_buf[0:QKV_HIST, :] = jnp.zeros((QKV_HIST, W), F32)

    @pl.when(t > 0)
    def _():
        y_buf[0:CONF_HIST, :] = y_buf[tl:tl + CONF_HIST, :]
        cc_buf[0:QKV_HIST, :] = cc_buf[tl:tl + QKV_HIST, :]

    _rms_rows(x_ref, ng_ref, h_s, n_blocks)
    h = h_s[...]
    for i in range(7):
        p_s[:, i * W:(i + 1) * W] = _dot(h, wbc_ref[:, i * W:(i + 1) * W])

    def glu_body(rb, carry):
        r0 = pl.multiple_of(rb * CHUNK, CHUNK)
        rows = pl.ds(r0, CHUNK)
        y_buf[pl.ds(r0 + CONF_HIST, CHUNK), :] = p_s[rows, U] * _sigmoid(p_s[rows, UG])
        cc_buf[pl.ds(r0 + QKV_HIST, CHUNK), :] = p_s[rows, CC] * p_s[rows, CH]
        return carry
    lax.fori_loop(0, n_blocks, glu_body, 0)

    def conv_body(rb, carry):
        r0 = pl.multiple_of(rb * CHUNK, CHUNK)
        rows = pl.ds(r0, CHUNK)
        cols = []
        for cb in range(W // LANES):
            cs = slice(cb * LANES, (cb + 1) * LANES)
            win = y_buf[pl.ds(r0, CHUNK + CONF_HIST), cs]
            acc = None
            for j in range(CONF_KERNEL):
                o = CONF_HIST - (CONF_KERNEL - 1) + j
                term = win[o:o + CHUNK] * dww_ref[j:j + 1, cs]
                acc = term if acc is None else acc + term
            cols.append(acc + dwb_ref[:, cs])
        y = jnp.concatenate(cols, axis=1)
        mu = jnp.mean(y, axis=-1, keepdims=True)
        yc = y - mu
        var = jnp.mean(yc * yc, axis=-1, keepdims=True)
        yn = yc * lax.rsqrt(var + NORM_EPS) * lng_ref[...] + lnb_ref[...]
        yn_s[rows, :] = _silu(yn).astype(BF16)
        for cb in range(W // LANES):
            cs = slice(cb * LANES, (cb + 1) * LANES)
            win = cc_buf[pl.ds(r0, CHUNK + QKV_HIST), cs]
            acc = None
            for j in range(SC_KERNEL):
                o = QKV_HIST - (SC_KERNEL - 1) + j
                term = win[o:o + CHUNK] * scw_ref[j:j + 1, cs]
                acc = term if acc is None else acc + term
            pc = slice(3 * W + cb * LANES, 3 * W + (cb + 1) * LANES)
            pz = slice(6 * W + cb * LANES, 6 * W + (cb + 1) * LANES)
            yc3 = p_s[rows, pc] * acc * _silu(p_s[rows, pz])
            out_ref[rows, W + cb * LANES:W + (cb + 1) * LANES] = yc3.astype(BF16)
        return carry
    lax.fori_loop(0, n_blocks, conv_body, 0)

    z = _dot(yn_s[...], pww_ref[...]) + pwb_ref[...]
    out_ref[:, 0:W] = (z * _silu(p_s[:, ZB])).astype(BF16)


def _out_kernel(x_ref, ma_ref, mbc_ref, wo_ref, fg_ref, out_ref, *, tl, pad, final):
    t = pl.program_id(1)
    y = x_ref[...] + _dot(ma_ref[...], wo_ref[0:GDN_WIDTH, :]) \
        + _dot(mbc_ref[...], wo_ref[GDN_WIDTH:D_MIX, :])
    if final:
        y = y * lax.rsqrt(jnp.mean(y * y, axis=-1, keepdims=True) + NORM_EPS) * fg_ref[...]
    row = lax.broadcasted_iota(jnp.int32, (tl, LANES), 0) + t * tl
    live = (row >= pad)[:, 0:1]
    out_ref[...] = jnp.where(live, y, 0.0)


def _const_spec(shape):
    nd = len(shape)
    return pl.BlockSpec(shape, lambda b, t: (0,) * nd, pipeline_mode=pl.Buffered(1))


def _row_spec(tl, width):
    return pl.BlockSpec((None, tl, width), lambda b, t: (b, t, 0))


def _params():
    return pltpu.CompilerParams(dimension_semantics=("arbitrary", "arbitrary"),
                                vmem_limit_bytes=VMEM_LIMIT_BYTES)


def _pick_tile(lp):
    for tl in (192, 128, 64):
        if lp % tl == 0:
            return tl
    raise ValueError(f"padded sequence length {lp} is not a multiple of {CHUNK}")


def kernel(x, meta_tokens, norm_g, w_in, conv_qkv, a_log, dt_bias, gdn_norm_g, conf_dw_w, conf_dw_b, conf_ln_g, conf_ln_b, conf_pw_w, conf_pw_b, sc_conv_w, w_out, final_norm_g):
    bsz, seq, d = x.shape
    depth = w_in.shape[0]
    assert d == D_MODEL
    length = N_META + seq
    pad = (-length) % CHUNK
    lp = length + pad
    tl = _pick_tile(lp)
    nt = lp // tl
    grid = (bsz, nt)

    meta = jnp.broadcast_to(meta_tokens.astype(x.dtype)[None], (bsz, N_META, d))
    hcur = jnp.concatenate([jnp.zeros((bsz, pad, d), x.dtype), meta, x], axis=1)

    o_z = 3 * GDN_WIDTH
    o_b = o_z + GDN_WIDTH
    o_a = o_b + GDN_HEADS
    o_u = o_a + GDN_HEADS
    w_a = w_in[:, :, 0:o_b].astype(BF16)
    w_ba = jnp.zeros((depth, d, 2 * LANES), F32)
    w_ba = w_ba.at[:, :, 0:GDN_HEADS].set(w_in[:, :, o_b:o_a])
    w_ba = w_ba.at[:, :, LANES:LANES + GDN_HEADS].set(w_in[:, :, o_a:o_u]).astype(BF16)
    w_bc = w_in[:, :, o_u:].astype(BF16)
    w_o = w_out.astype(BF16)
    pw_w = conf_pw_w.astype(BF16)
    lane_pad = ((0, 0), (0, LANES - GDN_HEADS))
    alog_p = jnp.pad(a_log.astype(F32), lane_pad)[:, None, :]
    dtb_p = jnp.pad(dt_bias.astype(F32), lane_pad)[:, None, :]
    row2 = lambda a: a.astype(F32)[:, None, :]

    gdn_call = pl.pallas_call(
        functools.partial(_gdn_kernel, tl=tl, pad=pad),
        grid=grid,
        in_specs=[_row_spec(tl, d), _const_spec((1, d)), _const_spec((d, o_b)),
                  _const_spec((d, 2 * LANES)), _const_spec((GDN_CONV, 3 * GDN_WIDTH)),
                  _const_spec((1, LANES)), _const_spec((1, LANES)), _const_spec((1, LANES))],
        out_specs=_row_spec(tl, GDN_WIDTH),
        out_shape=jax.ShapeDtypeStruct((bsz, lp, GDN_WIDTH), BF16),
        scratch_shapes=[pltpu.VMEM((tl, d), BF16),
                        pltpu.VMEM((tl + QKV_HIST, 3 * GDN_WIDTH), F32),
                        pltpu.VMEM((tl, GDN_WIDTH), F32), pltpu.VMEM((tl, GDN_WIDTH), F32),
                        pltpu.VMEM((tl, GDN_WIDTH), F32), pltpu.VMEM((tl, GDN_WIDTH), F32),
                        pltpu.VMEM((tl, LANES), F32), pltpu.VMEM((tl, LANES), F32),
                        pltpu.VMEM((GDN_HEADS, GDN_HEAD_DIM, GDN_HEAD_DIM), F32)],
        compiler_params=_params(), name="gdn_mixer")

    conv_call = pl.pallas_call(
        functools.partial(_conv_kernel, tl=tl),
        grid=grid,
        in_specs=[_row_spec(tl, d), _const_spec((1, d)), _const_spec((d, 7 * CONF_WIDTH)),
                  _const_spec((CONF_KERNEL, CONF_WIDTH)), _const_spec((1, CONF_WIDTH)),
                  _const_spec((1, CONF_WIDTH)), _const_spec((1, CONF_WIDTH)),
                  _const_spec((CONF_WIDTH, CONF_WIDTH)), _const_spec((1, CONF_WIDTH)),
                  _const_spec((SC_KERNEL, SC_WIDTH))],
        out_specs=_row_spec(tl, CONF_WIDTH + SC_WIDTH),
        out_shape=jax.ShapeDtypeStruct((bsz, lp, CONF_WIDTH + SC_WIDTH), BF16),
        scratch_shapes=[pltpu.VMEM((tl, d), BF16),
                        pltpu.VMEM((tl, 7 * CONF_WIDTH), F32),
                        pltpu.VMEM((tl + CONF_HIST, CONF_WIDTH), F32),
                        pltpu.VMEM((tl + QKV_HIST, SC_WIDTH), F32),
                        pltpu.VMEM((tl, CONF_WIDTH), BF16)],
        compiler_params=_params(), name="conv_mixers")

    def out_call(final):
        return pl.pallas_call(
            functools.partial(_out_kernel, tl=tl, pad=pad, final=final),
            grid=grid,
            in_specs=[_row_spec(tl, d), _row_spec(tl, GDN_WIDTH),
                      _row_spec(tl, CONF_WIDTH + SC_WIDTH), _const_spec((D_MIX, d)),
                      _const_spec((1, d))],
            out_specs=_row_spec(tl, d),
            out_shape=jax.ShapeDtypeStruct((bsz, lp, d), F32),
            compiler_params=_params(), name="out_proj")

    fg = final_norm_g.astype(F32)[None, :]
    for l in range(depth):
        ng = norm_g[l].astype(F32)[None, :]
        mix_a = gdn_call(hcur, ng, w_a[l], w_ba[l], conv_qkv[l].astype(F32), alog_p[l], dtb_p[l],
                         gdn_norm_g[l].astype(F32)[None, :])
        mix_bc = conv_call(hcur, ng, w_bc[l], conf_dw_w[l].astype(F32), row2(conf_dw_b)[l],
                           row2(conf_ln_g)[l], row2(conf_ln_b)[l], pw_w[l], row2(conf_pw_b)[l],
                           sc_conv_w[l].astype(F32))
        hcur = out_call(l == depth - 1)(hcur, mix_a, mix_bc, w_o[l], fg)
    return hcur[:, pad + N_META:]
```

```python
import functools

import jax
import jax.numpy as jnp
from jax import lax
from jax.experimental import pallas as pl
from jax.experimental.pallas import tpu as pltpu

F32 = jnp.float32
BF16 = jnp.bfloat16

D_MODEL = 1024
N_META = 16
CHUNK = 64
NORM_EPS = 1e-6
GDN_HEADS = 8
GDN_HEAD_DIM = 128
GDN_WIDTH = GDN_HEADS * GDN_HEAD_DIM
GDN_CONV = 4
CONF_WIDTH = 512
CONF_KERNEL = 31
SC_WIDTH = 512
SC_KERNEL = 3
D_MIX = GDN_WIDTH + CONF_WIDTH + SC_WIDTH

HEAD_GROUP = 4
SUBLANES = 8
LANES = 128
QKV_HIST = 8
CONF_HIST = 32
VMEM_LIMIT_BYTES = 48 * 1024 * 1024


def _sigmoid(x):
    return 1.0 / (1.0 + jnp.exp(-x))


def _silu(x):
    return x * _sigmoid(x)


def _dot(a, b):
    return jnp.dot(a, b, preferred_element_type=F32)


def _dot_exact(a, b, dims):
    return lax.dot_general(a, b, (dims, ((), ())), precision=lax.Precision.HIGHEST,
                           preferred_element_type=F32)


def _rms_rows(x_ref, g_ref, h_ref, n_blocks):
    def body(rb, carry):
        r0 = pl.multiple_of(rb * CHUNK, CHUNK)
        x = x_ref[pl.ds(r0, CHUNK), :]
        ms = jnp.mean(x * x, axis=-1, keepdims=True)
        h_ref[pl.ds(r0, CHUNK), :] = (x * lax.rsqrt(ms + NORM_EPS) * g_ref[...]).astype(BF16)
        return carry
    lax.fori_loop(0, n_blocks, body, 0)


def _bcast_heads_128(a):
    r = a.shape[0]
    return jnp.concatenate([jnp.broadcast_to(a[:, h:h + 1], (r, LANES)) for h in range(GDN_HEADS)],
                           axis=1)


def _bcast_heads_64(a, left):
    r = a.shape[0]
    return jnp.concatenate(
        [jnp.where(left, jnp.broadcast_to(a[:, 2 * p:2 * p + 1], (r, LANES)),
                   jnp.broadcast_to(a[:, 2 * p + 1:2 * p + 2], (r, LANES)))
         for p in range(GDN_HEADS // 2)], axis=1)


def _block_diag(x16, mask):
    return jnp.concatenate([x16] * HEAD_GROUP, axis=0) * mask


def _gdn_kernel(x_ref, ng_ref, wa_ref, wba_ref, cw_ref, alog_ref, dtb_ref, gng_ref, m512_ref, m256_ref,
                out_ref, h_s, qkv_buf, q_s, k_s, v_s, za_s, beta_s, g_s, st_ref,
                gc_s, gct_s, eg_s, q16_s, k16_s, kb_s, vb_s, kbg_s, qd_s, kd_s, qk_s, u_s, w_s,
                *, tl, pad):
    t = pl.program_id(1)
    n_blocks = tl // CHUNK

    @pl.when(t == 0)
    def _():
        qkv_buf[0:QKV_HIST, :] = jnp.zeros((QKV_HIST, 3 * GDN_WIDTH), F32)
        st_ref[...] = jnp.zeros_like(st_ref)

    @pl.when(t > 0)
    def _():
        qkv_buf[0:QKV_HIST, :] = qkv_buf[tl:tl + QKV_HIST, :]

    _rms_rows(x_ref, ng_ref, h_s, n_blocks)

    h = h_s[...]
    for n0 in range(0, 3 * GDN_WIDTH, 512):
        qkv_buf[QKV_HIST:QKV_HIST + tl, n0:n0 + 512] = _dot(h, wa_ref[:, n0:n0 + 512])
    for n0 in range(0, GDN_WIDTH, 512):
        za_s[:, n0:n0 + 512] = _dot(h, wa_ref[:, 3 * GDN_WIDTH + n0:3 * GDN_WIDTH + n0 + 512])
    ba = _dot(h, wba_ref[...])
    row = lax.broadcasted_iota(jnp.int32, (tl, LANES), 0) + t * tl
    live = row >= pad
    beta_s[...] = jnp.where(live, _sigmoid(ba[:, 0:LANES]), 0.0)
    a_in = ba[:, LANES:2 * LANES] + dtb_ref[...]
    softplus = jnp.maximum(a_in, 0.0) + jnp.log1p(jnp.exp(-jnp.abs(a_in)))
    g_s[...] = jnp.where(live, -jnp.exp(alog_ref[...]) * softplus, 0.0)

    def conv_body(rb, carry):
        r0 = pl.multiple_of(rb * CHUNK, CHUNK)
        for cb in range(3 * GDN_HEADS):
            cs = slice(cb * LANES, (cb + 1) * LANES)
            win = qkv_buf[pl.ds(r0, CHUNK + QKV_HIST), cs]
            acc = None
            for j in range(GDN_CONV):
                o = QKV_HIST - (GDN_CONV - 1) + j
                term = win[o:o + CHUNK] * cw_ref[j:j + 1, cs]
                acc = term if acc is None else acc + term
            y = _silu(acc)
            if cb < 2 * GDN_HEADS:
                y = y * lax.rsqrt(jnp.sum(y * y, axis=-1, keepdims=True) + NORM_EPS)
            if cb < GDN_HEADS:
                y = y * (GDN_HEAD_DIM ** -0.5)
            dst = (q_s, k_s, v_s)[cb // GDN_HEADS]
            hc = cb % GDN_HEADS
            dst[pl.ds(r0, CHUNK), hc * LANES:(hc + 1) * LANES] = y
        return carry
    lax.fori_loop(0, n_blocks, conv_body, 0)

    gw = HEAD_GROUP * LANES
    gc4 = HEAD_GROUP * CHUNK
    n_groups = GDN_HEADS // HEAD_GROUP
    ri = lax.broadcasted_iota(jnp.int32, (CHUNK, CHUNK), 0)
    ci = lax.broadcasted_iota(jnp.int32, (CHUNK, CHUNK), 1)
    tri_lo = (ri >= ci).astype(F32)
    ri2 = lax.broadcasted_iota(jnp.int32, (CHUNK, LANES), 0)
    ci2 = lax.broadcasted_iota(jnp.int32, (CHUNK, LANES), 1)
    tri_up2 = (ri2 <= (ci2 % CHUNK)).astype(F32)
    left = ci2 < CHUNK
    rt = lax.broadcasted_iota(jnp.int32, (CHUNK, gc4), 0)
    lt = lax.broadcasted_iota(jnp.int32, (CHUNK, gc4), 1) % CHUNK
    incl_t = rt >= lt
    strict_t = rt > lt
    eye_t = (rt == lt).astype(F32)
    nt_dims = (((1,), (1,)), ((), ()))
    tn_dims = (((0,), (0,)), ((), ()))
    chunk_rows = [slice(c * CHUNK, (c + 1) * CHUNK) for c in range(n_blocks)]

    for c, rows in enumerate(chunk_rows):
        g_c = g_s[rows, :]
        gc = _dot_exact(tri_lo, g_c, ((1,), (0,)))
        gct = _dot_exact(g_c, tri_up2, ((0,), (0,)))
        gc_s[rows, :] = gc
        gct_s[c * SUBLANES:(c + 1) * SUBLANES, :] = gct[0:SUBLANES, :]
        eg = jnp.exp(gc)
        eg_s[rows, :] = eg
        edec = jnp.exp(gc[CHUNK - 1:CHUNK, :] - gc)
        beta_c = beta_s[rows, :]
        for hd in range(GDN_HEADS):
            cs = slice(hd * LANES, (hd + 1) * LANES)
            q = q_s[rows, cs]
            k = k_s[rows, cs]
            bcol = beta_c[:, hd:hd + 1]
            ecol = eg[:, hd:hd + 1]
            kb = k * bcol
            q16_s[rows, cs] = q.astype(BF16)
            k16_s[rows, cs] = k.astype(BF16)
            kb_s[rows, cs] = kb.astype(BF16)
            vb_s[rows, cs] = (v_s[rows, cs] * bcol).astype(BF16)
            kbg_s[rows, cs] = (kb * ecol).astype(BF16)
            qd_s[rows, cs] = (q * ecol).astype(BF16)
            kd_s[rows, cs] = (k * edec[:, hd:hd + 1]).astype(BF16)

    chains = [(c, g) for c in range(n_blocks) for g in range(n_groups)]
    tmat, xmat = {}, {}
    for c, rows in enumerate(chunk_rows):
        cexp = _bcast_heads_64(gc_s[rows, :], left)
        gt = gct_s[c * SUBLANES:(c + 1) * SUBLANES, :]
        rrow = jnp.concatenate(
            [jnp.where(left[0:1], gt[2 * p:2 * p + 1], gt[2 * p + 1:2 * p + 2])
             for p in range(GDN_HEADS // 2)], axis=1)
        dec_all = jnp.exp(cexp - rrow)
        for g in range(n_groups):
            cg = slice(g * gw, (g + 1) * gw)
            cl = slice(g * gc4, (g + 1) * gc4)
            lhs = jnp.concatenate([q16_s[rows, cg], kb_s[rows, cg]], axis=0)
            qkkk = lax.dot_general(lhs, _block_diag(k16_s[rows, cg], m512_ref[...]), nt_dims,
                                   preferred_element_type=F32)
            dec = jnp.where(incl_t, dec_all[:, cl], 0.0)
            qk_s[rows, cl] = (qkkk[0:CHUNK] * dec).astype(BF16)
            nmat = jnp.where(strict_t, qkkk[CHUNK:2 * CHUNK] * dec, 0.0)
            tmat[c, g] = eye_t - nmat
            xmat[c, g] = nmat

    for key in chains:
        x16 = xmat[key].astype(BF16)
        xmat[key] = _dot(x16, _block_diag(x16, m256_ref[...]))
    for _ in range(4):
        for key in chains:
            x16 = xmat[key].astype(BF16)
            r = _dot(jnp.concatenate([tmat[key].astype(BF16), x16], axis=0),
                     _block_diag(x16, m256_ref[...]))
            tmat[key] = tmat[key] + r[0:CHUNK]
            xmat[key] = r[CHUNK:2 * CHUNK]
    for key in chains:
        x16 = xmat[key].astype(BF16)
        tmat[key] = tmat[key] + _dot(tmat[key].astype(BF16), _block_diag(x16, m256_ref[...]))

    for c, g in chains:
        rows = chunk_rows[c]
        cg = slice(g * gw, (g + 1) * gw)
        t16 = tmat[c, g].astype(BF16)
        u_s[rows, cg] = _dot(t16, _block_diag(vb_s[rows, cg], m512_ref[...]))
        w_s[rows, cg] = _dot(t16, _block_diag(kbg_s[rows, cg], m512_ref[...])).astype(BF16)

    heads = range(GDN_HEADS)
    for c, rows in enumerate(chunk_rows):
        hs = [slice(hd * LANES, (hd + 1) * LANES) for hd in heads]
        egl = eg_s[c * CHUNK + CHUNK - 1:(c + 1) * CHUNK, :]
        wq = [_dot(jnp.concatenate([w_s[rows, hs[hd]], qd_s[rows, hs[hd]]], axis=0),
                   st_ref[hd].astype(BF16)) for hd in heads]
        vn16 = [(u_s[rows, hs[hd]] - wq[hd][0:CHUNK]).astype(BF16) for hd in heads]
        og = [_dot(qk_s[rows, g * gc4:(g + 1) * gc4],
                   _block_diag(jnp.concatenate(vn16[g * HEAD_GROUP:(g + 1) * HEAD_GROUP], axis=1),
                               m512_ref[...])) for g in range(n_groups)]
        for hd in heads:
            st_ref[hd] = st_ref[hd] * egl[:, hd:hd + 1] + lax.dot_general(
                kd_s[rows, hs[hd]], vn16[hd], tn_dims, preferred_element_type=F32)
        for hd in heads:
            hg = hd % HEAD_GROUP
            o = wq[hd][CHUNK:2 * CHUNK] + og[hd // HEAD_GROUP][:, hg * LANES:(hg + 1) * LANES]
            o = o * lax.rsqrt(jnp.mean(o * o, axis=-1, keepdims=True) + NORM_EPS)
            o = o * gng_ref[...] * _silu(za_s[rows, hs[hd]])
            out_ref[rows, hs[hd]] = o.astype(BF16)


def _conv_kernel(x_ref, ng_ref, wbc_ref, dww_ref, dwb_ref, lng_ref, lnb_ref, pww_ref, pwb_ref,
                 scw_ref, out_ref, h_s, p_s, y_buf, cc_buf, yn_s, *, tl):
    t = pl.program_id(1)
    n_blocks = tl // CHUNK
    W = CONF_WIDTH
    U, UG, ZB, CB, CC, CH, ZC = (slice(i * W, (i + 1) * W) for i in range(7))

    @pl.when(t == 0)
    def _():
        y_buf[0:CONF_HIST, :] = jnp.zeros((CONF_HIST, W), F32)
        cc_buf[0:QKV_HIST, :] = jnp.zeros((QKV_HIST, W), F32)

    @pl.when(t > 0)
    def _():
        y_buf[0:CONF_HIST, :] = y_buf[tl:tl + CONF_HIST, :]
        cc_buf[0:QKV_HIST, :] = cc_buf[tl:tl + QKV_HIST, :]

    _rms_rows(x_ref, ng_ref, h_s, n_blocks)
    h = h_s[...]
    for i in range(7):
        p_s[:, i * W:(i + 1) * W] = _dot(h, wbc_ref[:, i * W:(i + 1) * W])

    def glu_body(rb, carry):
        r0 = pl.multiple_of(rb * CHUNK, CHUNK)
        rows = pl.ds(r0, CHUNK)
        y_buf[pl.ds(r0 + CONF_HIST, CHUNK), :] = p_s[rows, U] * _sigmoid(p_s[rows, UG])
        cc_buf[pl.ds(r0 + QKV_HIST, CHUNK), :] = p_s[rows, CC] * p_s[rows, CH]
        return carry
    lax.fori_loop(0, n_blocks, glu_body, 0)

    def conv_body(rb, carry):
        r0 = pl.multiple_of(rb * CHUNK, CHUNK)
        rows = pl.ds(r0, CHUNK)
        cols = []
        for cb in range(W // LANES):
            cs = slice(cb * LANES, (cb + 1) * LANES)
            win = y_buf[pl.ds(r0, CHUNK + CONF_HIST), cs]
            acc = None
            for j in range(CONF_KERNEL):
                o = CONF_HIST - (CONF_KERNEL - 1) + j
                term = win[o:o + CHUNK] * dww_ref[j:j + 1, cs]
                acc = term if acc is None else acc + term
            cols.append(acc + dwb_ref[:, cs])
        y = jnp.concatenate(cols, axis=1)
        mu = jnp.mean(y, axis=-1, keepdims=True)
        yc = y - mu
        var = jnp.mean(yc * yc, axis=-1, keepdims=True)
        yn = yc * lax.rsqrt(var + NORM_EPS) * lng_ref[...] + lnb_ref[...]
        yn_s[rows, :] = _silu(yn).astype(BF16)
        for cb in range(W // LANES):
            cs = slice(cb * LANES, (cb + 1) * LANES)
            win = cc_buf[pl.ds(r0, CHUNK + QKV_HIST), cs]
            acc = None
            for j in range(SC_KERNEL):
                o = QKV_HIST - (SC_KERNEL - 1) + j
                term = win[o:o + CHUNK] * scw_ref[j:j + 1, cs]
                acc = term if acc is None else acc + term
            pc = slice(3 * W + cb * LANES, 3 * W + (cb + 1) * LANES)
            pz = slice(6 * W + cb * LANES, 6 * W + (cb + 1) * LANES)
            yc3 = p_s[rows, pc] * acc * _silu(p_s[rows, pz])
            out_ref[rows, W + cb * LANES:W + (cb + 1) * LANES] = yc3.astype(BF16)
        return carry
    lax.fori_loop(0, n_blocks, conv_body, 0)

    z = _dot(yn_s[...], pww_ref[...]) + pwb_ref[...]
    out_ref[:, 0:W] = (z * _silu(p_s[:, ZB])).astype(BF16)


def _out_kernel(x_ref, ma_ref, mbc_ref, wo_ref, fg_ref, out_ref, *, tl, pad, final):
    t = pl.program_id(1)
    y = x_ref[...] + _dot(ma_ref[...], wo_ref[0:GDN_WIDTH, :]) \
        + _dot(mbc_ref[...], wo_ref[GDN_WIDTH:D_MIX, :])
    if final:
        y = y * lax.rsqrt(jnp.mean(y * y, axis=-1, keepdims=True) + NORM_EPS) * fg_ref[...]
    row = lax.broadcasted_iota(jnp.int32, (tl, LANES), 0) + t * tl
    live = (row >= pad)[:, 0:1]
    out_ref[...] = jnp.where(live, y, 0.0)


def _const_spec(shape):
    nd = len(shape)
    return pl.BlockSpec(shape, lambda b, t: (0,) * nd, pipeline_mode=pl.Buffered(1))


def _row_spec(tl, width):
    return pl.BlockSpec((None, tl, width), lambda b, t: (b, t, 0))


def _params():
    return pltpu.CompilerParams(dimension_semantics=("arbitrary", "arbitrary"),
                                vmem_limit_bytes=VMEM_LIMIT_BYTES)


def _pick_tile(lp):
    for tl in (192, 128, 64):
        if lp % tl == 0:
            return tl
    raise ValueError(f"padded sequence length {lp} is not a multiple of {CHUNK}")


def kernel(x, meta_tokens, norm_g, w_in, conv_qkv, a_log, dt_bias, gdn_norm_g, conf_dw_w, conf_dw_b, conf_ln_g, conf_ln_b, conf_pw_w, conf_pw_b, sc_conv_w, w_out, final_norm_g):
    bsz, seq, d = x.shape
    depth = w_in.shape[0]
    assert d == D_MODEL
    length = N_META + seq
    pad = (-length) % CHUNK
    lp = length + pad
    tl = _pick_tile(lp)
    nt = lp // tl
    grid = (bsz, nt)

    meta = jnp.broadcast_to(meta_tokens.astype(x.dtype)[None], (bsz, N_META, d))
    hcur = jnp.concatenate([jnp.zeros((bsz, pad, d), x.dtype), meta, x], axis=1)

    o_z = 3 * GDN_WIDTH
    o_b = o_z + GDN_WIDTH
    o_a = o_b + GDN_HEADS
    o_u = o_a + GDN_HEADS
    w_a = w_in[:, :, 0:o_b].astype(BF16)
    w_ba = jnp.zeros((depth, d, 2 * LANES), F32)
    w_ba = w_ba.at[:, :, 0:GDN_HEADS].set(w_in[:, :, o_b:o_a])
    w_ba = w_ba.at[:, :, LANES:LANES + GDN_HEADS].set(w_in[:, :, o_a:o_u]).astype(BF16)
    w_bc = w_in[:, :, o_u:].astype(BF16)
    w_o = w_out.astype(BF16)
    pw_w = conf_pw_w.astype(BF16)
    lane_pad = ((0, 0), (0, LANES - GDN_HEADS))
    alog_p = jnp.pad(a_log.astype(F32), lane_pad)[:, None, :]
    dtb_p = jnp.pad(dt_bias.astype(F32), lane_pad)[:, None, :]
    row2 = lambda a: a.astype(F32)[:, None, :]
    blk = jnp.arange(HEAD_GROUP * CHUNK)[:, None] // CHUNK
    m512 = (blk == jnp.arange(HEAD_GROUP * LANES)[None, :] // LANES).astype(BF16)
    m256 = (blk == jnp.arange(HEAD_GROUP * CHUNK)[None, :] // CHUNK).astype(BF16)

    gdn_call = pl.pallas_call(
        functools.partial(_gdn_kernel, tl=tl, pad=pad),
        grid=grid,
        in_specs=[_row_spec(tl, d), _const_spec((1, d)), _const_spec((d, o_b)),
                  _const_spec((d, 2 * LANES)), _const_spec((GDN_CONV, 3 * GDN_WIDTH)),
                  _const_spec((1, LANES)), _const_spec((1, LANES)), _const_spec((1, LANES)),
                  _const_spec((HEAD_GROUP * CHUNK, HEAD_GROUP * LANES)),
                  _const_spec((HEAD_GROUP * CHUNK, HEAD_GROUP * CHUNK))],
        out_specs=_row_spec(tl, GDN_WIDTH),
        out_shape=jax.ShapeDtypeStruct((bsz, lp, GDN_WIDTH), BF16),
        scratch_shapes=[pltpu.VMEM((tl, d), BF16),
                        pltpu.VMEM((tl + QKV_HIST, 3 * GDN_WIDTH), F32),
                        pltpu.VMEM((tl, GDN_WIDTH), F32), pltpu.VMEM((tl, GDN_WIDTH), F32),
                        pltpu.VMEM((tl, GDN_WIDTH), F32), pltpu.VMEM((tl, GDN_WIDTH), F32),
                        pltpu.VMEM((tl, LANES), F32), pltpu.VMEM((tl, LANES), F32),
                        pltpu.VMEM((GDN_HEADS, GDN_HEAD_DIM, GDN_HEAD_DIM), F32),
                        pltpu.VMEM((tl, LANES), F32),
                        pltpu.VMEM((tl // CHUNK * SUBLANES, LANES), F32),
                        pltpu.VMEM((tl, LANES), F32),
                        ] + [pltpu.VMEM((tl, GDN_WIDTH), BF16)] * 7
                        + [pltpu.VMEM((tl, GDN_WIDTH // 2), BF16),
                           pltpu.VMEM((tl, GDN_WIDTH), F32),
                           pltpu.VMEM((tl, GDN_WIDTH), BF16)],
        compiler_params=_params(), name="gdn_mixer")

    conv_call = pl.pallas_call(
        functools.partial(_conv_kernel, tl=tl),
        grid=grid,
        in_specs=[_row_spec(tl, d), _const_spec((1, d)), _const_spec((d, 7 * CONF_WIDTH)),
                  _const_spec((CONF_KERNEL, CONF_WIDTH)), _const_spec((1, CONF_WIDTH)),
                  _const_spec((1, CONF_WIDTH)), _const_spec((1, CONF_WIDTH)),
                  _const_spec((CONF_WIDTH, CONF_WIDTH)), _const_spec((1, CONF_WIDTH)),
                  _const_spec((SC_KERNEL, SC_WIDTH))],
        out_specs=_row_spec(tl, CONF_WIDTH + SC_WIDTH),
        out_shape=jax.ShapeDtypeStruct((bsz, lp, CONF_WIDTH + SC_WIDTH), BF16),
        scratch_shapes=[pltpu.VMEM((tl, d), BF16),
                        pltpu.VMEM((tl, 7 * CONF_WIDTH), F32),
                        pltpu.VMEM((tl + CONF_HIST, CONF_WIDTH), F32),
                        pltpu.VMEM((tl + QKV_HIST, SC_WIDTH), F32),
                        pltpu.VMEM((tl, CONF_WIDTH), BF16)],
        compiler_params=_params(), name="conv_mixers")

    def out_call(final):
        return pl.pallas_call(
            functools.partial(_out_kernel, tl=tl, pad=pad, final=final),
            grid=grid,
            in_specs=[_row_spec(tl, d), _row_spec(tl, GDN_WIDTH),
                      _row_spec(tl, CONF_WIDTH + SC_WIDTH), _const_spec((D_MIX, d)),
                      _const_spec((1, d))],
            out_specs=_row_spec(tl, d),
            out_shape=jax.ShapeDtypeStruct((bsz, lp, d), F32),
            compiler_params=_params(), name="out_proj")

    fg = final_norm_g.astype(F32)[None, :]
    for l in range(depth):
        ng = norm_g[l].astype(F32)[None, :]
        mix_a = gdn_call(hcur, ng, w_a[l], w_ba[l], conv_qkv[l].astype(F32), alog_p[l], dtb_p[l],
                         gdn_norm_g[l].astype(F32)[None, :], m512, m256)
        mix_bc = conv_call(hcur, ng, w_bc[l], conf_dw_w[l].astype(F32), row2(conf_dw_b)[l],
                           row2(conf_ln_g)[l], row2(conf_ln_b)[l], pw_w[l], row2(conf_pw_b)[l],
                           sc_conv_w[l].astype(F32))
        hcur = out_call(l == depth - 1)(hcur, mix_a, mix_bc, w_o[l], fg)
    return hcur[:, pad + N_META:]
```

```python
import functools
import math

import jax
import jax.numpy as jnp
from jax import lax
from jax.experimental import pallas as pl
from jax.experimental.pallas import tpu as pltpu

F32 = jnp.float32
BF16 = jnp.bfloat16

D_MODEL = 1024
N_META = 16
CHUNK = 64
NORM_EPS = 1e-6
GDN_HEADS = 8
GDN_HEAD_DIM = 128
GDN_WIDTH = GDN_HEADS * GDN_HEAD_DIM
GDN_CONV = 4
CONF_WIDTH = 512
CONF_KERNEL = 31
SC_WIDTH = 512
SC_KERNEL = 3
D_MIX = GDN_WIDTH + CONF_WIDTH + SC_WIDTH

HEAD_GROUP = 4
SUBLANES = 8
LANES = 128
QKV_HIST = 8
CONF_HIST = 32
VMEM_LIMIT_BYTES = 56 * 1024 * 1024
LOG2E = math.log2(math.e)


def _sigmoid(x):
    return 1.0 / (1.0 + jnp.exp2(x * (-LOG2E)))


def _silu(x):
    return x * _sigmoid(x)


def _dot(a, b):
    return jnp.dot(a, b, preferred_element_type=F32)


def _dot_exact(a, b, dims):
    return lax.dot_general(a, b, (dims, ((), ())), precision=lax.Precision.HIGHEST,
                           preferred_element_type=F32)


def _bcast_heads_64(a, left):
    r = a.shape[0]
    return jnp.concatenate(
        [jnp.where(left, jnp.broadcast_to(a[:, 2 * p:2 * p + 1], (r, LANES)),
                   jnp.broadcast_to(a[:, 2 * p + 1:2 * p + 2], (r, LANES)))
         for p in range(GDN_HEADS // 2)], axis=1)


def _masked_block_diag(x16, mask):
    return jnp.concatenate([x16] * HEAD_GROUP, axis=0) * mask


def _block_diag(blocks):
    zero = jnp.zeros_like(blocks[0])
    n = len(blocks)
    return jnp.concatenate(
        [jnp.concatenate([blocks[i] if i == j else zero for j in range(n)], axis=1)
         for i in range(n)], axis=0)


def _layer_kernel(x_ref, ng_ref, wa_ref, wba_ref, cw_ref, alog_ref, dtb_ref, gng_ref, m256_ref,
                  wbc_ref, dww_ref, dwb_ref, lng_ref, lnb_ref, pww_ref, pwb_ref, scw_ref,
                  wo_ref, fg_ref,
                  out_ref,
                  h_s, qkv_buf, q_s, k_s, v_s, za_s, beta_s, g_s, st_ref, gc_s, gct_s, eg_s,
                  q16_s, k16_s, kb_s, vb_s, kbg_s, qd_s, kd_s, qk_s, u_s, w_s,
                  p_s, y_buf, cc_buf, yn_s, mix_s,
                  *, tl, pad, final):
    t = pl.program_id(1)
    n_blocks = tl // CHUNK
    W = CONF_WIDTH
    U, UG, ZB, CB, CC, CH, ZC = (slice(i * W, (i + 1) * W) for i in range(7))
    mix_b0 = GDN_WIDTH
    mix_c0 = GDN_WIDTH + CONF_WIDTH

    @pl.when(t == 0)
    def _():
        qkv_buf[0:QKV_HIST, :] = jnp.zeros((QKV_HIST, 3 * GDN_WIDTH), F32)
        y_buf[0:CONF_HIST, :] = jnp.zeros((CONF_HIST, W), F32)
        cc_buf[0:QKV_HIST, :] = jnp.zeros((QKV_HIST, W), F32)
        st_ref[...] = jnp.zeros_like(st_ref)

    @pl.when(t > 0)
    def _():
        qkv_buf[0:QKV_HIST, :] = qkv_buf[tl:tl + QKV_HIST, :]
        y_buf[0:CONF_HIST, :] = y_buf[tl:tl + CONF_HIST, :]
        cc_buf[0:QKV_HIST, :] = cc_buf[tl:tl + QKV_HIST, :]

    def rms_body(rb, carry):
        r0 = pl.multiple_of(rb * CHUNK, CHUNK)
        x = x_ref[pl.ds(r0, CHUNK), :]
        ms = jnp.mean(x * x, axis=-1, keepdims=True)
        h_s[pl.ds(r0, CHUNK), :] = (x * lax.rsqrt(ms + NORM_EPS) * ng_ref[...]).astype(BF16)
        return carry
    lax.fori_loop(0, n_blocks, rms_body, 0)
    h = h_s[...]


    pw = 4 * LANES

    def project_qkv(gi):
        qkv_buf[QKV_HIST:QKV_HIST + tl, gi * pw:(gi + 1) * pw] = _dot(h, wa_ref[:, gi * pw:(gi + 1) * pw])

    def conv_group(gi):
        for cb in range(gi * pw // LANES, (gi + 1) * pw // LANES):
            cs = slice(cb * LANES, (cb + 1) * LANES)
            for rb in range(n_blocks):
                r0 = rb * CHUNK
                acc = None
                for j in range(GDN_CONV):
                    o = r0 + QKV_HIST - (GDN_CONV - 1) + j
                    term = qkv_buf[o:o + CHUNK, cs] * cw_ref[j:j + 1, cs]
                    acc = term if acc is None else acc + term
                y = _silu(acc)
                if cb < 2 * GDN_HEADS:
                    scale = lax.rsqrt(jnp.sum(y * y, axis=-1, keepdims=True) + NORM_EPS)
                    if cb < GDN_HEADS:
                        scale = scale * (GDN_HEAD_DIM ** -0.5)
                    y = y * scale
                dst = (q_s, k_s, v_s)[cb // GDN_HEADS]
                hc = cb % GDN_HEADS
                dst[r0:r0 + CHUNK, hc * LANES:(hc + 1) * LANES] = y

    n_qkv_groups = 3 * GDN_WIDTH // pw
    project_qkv(0)
    for gi in range(n_qkv_groups):
        if gi + 1 < n_qkv_groups:
            project_qkv(gi + 1)
        conv_group(gi)
    for n0 in range(0, GDN_WIDTH, pw):
        za_s[:, n0:n0 + pw] = _dot(h, wa_ref[:, 3 * GDN_WIDTH + n0:3 * GDN_WIDTH + n0 + pw])
    ba = _dot(h, wba_ref[...])
    row = lax.broadcasted_iota(jnp.int32, (tl, LANES), 0) + t * tl
    live = row >= pad
    beta_s[...] = jnp.where(live, _sigmoid(ba[:, 0:LANES]), 0.0)
    a_in = ba[:, LANES:2 * LANES] + dtb_ref[...]
    softplus = jnp.maximum(a_in, 0.0) + jnp.log1p(jnp.exp(-jnp.abs(a_in)))
    g_s[...] = jnp.where(live, -jnp.exp(alog_ref[...]) * softplus, 0.0)

    def project_bc(cols):
        p_s[:, cols] = _dot(h, wbc_ref[:, cols])

    project_bc(U)
    project_bc(UG)
    y_buf[CONF_HIST:CONF_HIST + tl, :] = p_s[:, U] * _sigmoid(p_s[:, UG])
    project_bc(CC)
    project_bc(CH)
    cc_buf[QKV_HIST:QKV_HIST + tl, :] = p_s[:, CC] * p_s[:, CH]
    project_bc(CB)
    project_bc(ZC)
    project_bc(ZB)

    gw = HEAD_GROUP * LANES
    gc4 = HEAD_GROUP * CHUNK
    n_groups = GDN_HEADS // HEAD_GROUP
    ri = lax.broadcasted_iota(jnp.int32, (CHUNK, CHUNK), 0)
    ci = lax.broadcasted_iota(jnp.int32, (CHUNK, CHUNK), 1)
    tri_lo = (ri >= ci).astype(F32)
    ri2 = lax.broadcasted_iota(jnp.int32, (CHUNK, LANES), 0)
    ci2 = lax.broadcasted_iota(jnp.int32, (CHUNK, LANES), 1)
    tri_up2 = (ri2 <= (ci2 % CHUNK)).astype(F32)
    left = ci2 < CHUNK
    rt = lax.broadcasted_iota(jnp.int32, (CHUNK, gc4), 0)
    lt = lax.broadcasted_iota(jnp.int32, (CHUNK, gc4), 1) % CHUNK
    incl_t = rt >= lt
    strict_t = rt > lt
    eye_t = (rt == lt).astype(F32)
    nt_dims = (((1,), (1,)), ((), ()))
    tn_dims = (((0,), (0,)), ((), ()))
    chunk_rows = [slice(c * CHUNK, (c + 1) * CHUNK) for c in range(n_blocks)]
    heads = range(GDN_HEADS)
    hs = [slice(hd * LANES, (hd + 1) * LANES) for hd in heads]

    def head_blocks(ref, rows, g):
        return [ref[rows, hs[hd]] for hd in range(g * HEAD_GROUP, (g + 1) * HEAD_GROUP)]

    for c, rows in enumerate(chunk_rows):
        g_c = g_s[rows, :]
        gc = _dot_exact(tri_lo, g_c, ((1,), (0,)))
        gct = _dot_exact(g_c, tri_up2, ((0,), (0,)))
        gc_s[rows, :] = gc
        gct_s[c * SUBLANES:(c + 1) * SUBLANES, :] = gct[0:SUBLANES, :]
        eg = jnp.exp(gc)
        eg_s[rows, :] = eg
        edec = jnp.exp(gc[CHUNK - 1:CHUNK, :] - gc)
        beta_c = beta_s[rows, :]
        for hd in heads:
            cs = hs[hd]
            q = q_s[rows, cs]
            k = k_s[rows, cs]
            bcol = beta_c[:, hd:hd + 1]
            ecol = eg[:, hd:hd + 1]
            kb = k * bcol
            q16_s[rows, cs] = q.astype(BF16)
            k16_s[rows, cs] = k.astype(BF16)
            kb_s[rows, cs] = kb.astype(BF16)
            vb_s[rows, cs] = (v_s[rows, cs] * bcol).astype(BF16)
            kbg_s[rows, cs] = (kb * ecol).astype(BF16)
            qd_s[rows, cs] = (q * ecol).astype(BF16)
            kd_s[rows, cs] = (k * edec[:, hd:hd + 1]).astype(BF16)

    chains = [(c, g) for c in range(n_blocks) for g in range(n_groups)]
    tmat, xmat = {}, {}
    for c, rows in enumerate(chunk_rows):
        cexp = _bcast_heads_64(gc_s[rows, :], left)
        gt = gct_s[c * SUBLANES:(c + 1) * SUBLANES, :]
        rrow = jnp.concatenate(
            [jnp.where(left[0:1], gt[2 * p:2 * p + 1], gt[2 * p + 1:2 * p + 2])
             for p in range(GDN_HEADS // 2)], axis=1)
        dec_all = jnp.exp(cexp - rrow)
        for g in range(n_groups):
            cg = slice(g * gw, (g + 1) * gw)
            cl = slice(g * gc4, (g + 1) * gc4)
            lhs = jnp.concatenate([q16_s[rows, cg], kb_s[rows, cg]], axis=0)
            qkkk = lax.dot_general(lhs, _block_diag(head_blocks(k16_s, rows, g)), nt_dims,
                                   preferred_element_type=F32)
            dec = jnp.where(incl_t, dec_all[:, cl], 0.0)
            qk_s[rows, cl] = (qkkk[0:CHUNK] * dec).astype(BF16)
            nmat = jnp.where(strict_t, qkkk[CHUNK:2 * CHUNK] * dec, 0.0)
            tmat[c, g] = eye_t - nmat
            xmat[c, g] = nmat

    off31 = CONF_HIST - (CONF_KERNEL - 1)
    conv_cols = {}

    def conv31_block(rb, cb):
        r0 = rb * CHUNK
        cs = slice(cb * LANES, (cb + 1) * LANES)
        acc = None
        for b in range(SUBLANES):
            span = CHUNK + (SUBLANES if b else 0)
            z = None
            for a in range((CONF_KERNEL + off31 + SUBLANES - 1) // SUBLANES):
                j = SUBLANES * a + b - off31
                if 0 <= j < CONF_KERNEL:
                    term = y_buf[r0 + SUBLANES * a:r0 + SUBLANES * a + span, cs] * dww_ref[j:j + 1, cs]
                    z = term if z is None else z + term
            zs = z[b:b + CHUNK]
            acc = zs if acc is None else acc + zs
        conv_cols[rb, cb] = acc + dwb_ref[:, cs]
        if cb == W // LANES - 1:
            y = jnp.concatenate([conv_cols.pop((rb, i)) for i in range(W // LANES)], axis=1)
            mu = jnp.mean(y, axis=-1, keepdims=True)
            yc = y - mu
            var = jnp.mean(yc * yc, axis=-1, keepdims=True)
            yn = yc * lax.rsqrt(var + NORM_EPS) * lng_ref[...] + lnb_ref[...]
            yn_s[r0:r0 + CHUNK, :] = _silu(yn).astype(BF16)

    conv_blocks = [(rb, cb) for rb in range(n_blocks) for cb in range(W // LANES)]
    inv_steps = 6
    per_step = -(-len(conv_blocks) // inv_steps)

    def conv31_some(step):
        for blk in conv_blocks[step * per_step:(step + 1) * per_step]:
            conv31_block(*blk)

    for key in chains:
        x16 = xmat[key].astype(BF16)
        xmat[key] = _dot(x16, _masked_block_diag(x16, m256_ref[...]))
    conv31_some(0)
    for step in range(1, 5):
        for key in chains:
            x16 = xmat[key].astype(BF16)
            r = _dot(jnp.concatenate([tmat[key].astype(BF16), x16], axis=0),
                     _masked_block_diag(x16, m256_ref[...]))
            tmat[key] = tmat[key] + r[0:CHUNK]
            xmat[key] = r[CHUNK:2 * CHUNK]
        conv31_some(step)
    for key in chains:
        x16 = xmat[key].astype(BF16)
        tmat[key] = tmat[key] + _dot(tmat[key].astype(BF16), _masked_block_diag(x16, m256_ref[...]))
    conv31_some(5)

    for c, g in chains:
        rows = chunk_rows[c]
        cg = slice(g * gw, (g + 1) * gw)
        t16 = tmat[c, g].astype(BF16)
        u_s[rows, cg] = _dot(t16, _block_diag(head_blocks(vb_s, rows, g)))
        w_s[rows, cg] = _dot(t16, _block_diag(head_blocks(kbg_s, rows, g))).astype(BF16)

    zb = _dot(yn_s[...], pww_ref[...]) + pwb_ref[...]
    mix_s[:, mix_b0:mix_c0] = (zb * _silu(p_s[:, ZB])).astype(BF16)
    for rb in range(n_blocks):
        r0 = rb * CHUNK
        rows = slice(r0, r0 + CHUNK)
        for cb in range(W // LANES):
            cs = slice(cb * LANES, (cb + 1) * LANES)
            acc = None
            for j in range(SC_KERNEL):
                o = r0 + QKV_HIST - (SC_KERNEL - 1) + j
                term = cc_buf[o:o + CHUNK, cs] * scw_ref[j:j + 1, cs]
                acc = term if acc is None else acc + term
            pc = slice(3 * W + cb * LANES, 3 * W + (cb + 1) * LANES)
            pz = slice(6 * W + cb * LANES, 6 * W + (cb + 1) * LANES)
            yc3 = p_s[rows, pc] * acc * _silu(p_s[rows, pz])
            mix_s[rows, mix_c0 + cb * LANES:mix_c0 + (cb + 1) * LANES] = yc3.astype(BF16)

    out_ref[...] = x_ref[...] + _dot(mix_s[:, mix_b0:D_MIX], wo_ref[mix_b0:D_MIX, :])

    pairs = range(GDN_HEADS // 2)
    half = [slice(0, LANES), slice(LANES, 2 * LANES)]
    for c, rows in enumerate(chunk_rows):
        egl = eg_s[c * CHUNK + CHUNK - 1:(c + 1) * CHUNK, :]
        wq = [_dot(jnp.concatenate([w_s[rows, 2 * p * LANES:(2 * p + 2) * LANES],
                                    qd_s[rows, 2 * p * LANES:(2 * p + 2) * LANES]], axis=0),
                   _block_diag([st_ref[2 * p].astype(BF16), st_ref[2 * p + 1].astype(BF16)]))
              for p in pairs]
        vn16 = [(u_s[rows, hs[hd]] - wq[hd // 2][0:CHUNK, half[hd % 2]]).astype(BF16) for hd in heads]
        og = [_dot(qk_s[rows, p * LANES:(p + 1) * LANES], _block_diag([vn16[2 * p], vn16[2 * p + 1]]))
              for p in pairs]
        for hd in heads:
            st_ref[hd] = st_ref[hd] * egl[:, hd:hd + 1] + lax.dot_general(
                kd_s[rows, hs[hd]], vn16[hd], tn_dims, preferred_element_type=F32)
        for hd in heads:
            o = wq[hd // 2][CHUNK:2 * CHUNK, half[hd % 2]] + og[hd // 2][:, half[hd % 2]]
            o = o * lax.rsqrt(jnp.mean(o * o, axis=-1, keepdims=True) + NORM_EPS)
            o = o * gng_ref[...] * _silu(za_s[rows, hs[hd]])
            mix_s[rows, hs[hd]] = o.astype(BF16)

    y = out_ref[...] + _dot(mix_s[:, 0:GDN_WIDTH], wo_ref[0:GDN_WIDTH, :])
    if final:
        y = y * lax.rsqrt(jnp.mean(y * y, axis=-1, keepdims=True) + NORM_EPS) * fg_ref[...]
    out_ref[...] = jnp.where(live[:, 0:1], y, 0.0)


def _const_spec(shape):
    nd = len(shape)
    return pl.BlockSpec(shape, lambda b, t: (0,) * nd, pipeline_mode=pl.Buffered(1))


def _row_spec(tl, width):
    return pl.BlockSpec((None, tl, width), lambda b, t: (b, t, 0))


def _pick_tile(lp):
    for tl in (192, 128, 64):
        if lp % tl == 0:
            return tl
    raise ValueError(f"padded sequence length {lp} is not a multiple of {CHUNK}")


def kernel(x, meta_tokens, norm_g, w_in, conv_qkv, a_log, dt_bias, gdn_norm_g, conf_dw_w, conf_dw_b, conf_ln_g, conf_ln_b, conf_pw_w, conf_pw_b, sc_conv_w, w_out, final_norm_g):
    bsz, seq, d = x.shape
    depth = w_in.shape[0]
    assert d == D_MODEL
    length = N_META + seq
    pad = (-length) % CHUNK
    lp = length + pad
    tl = _pick_tile(lp)
    grid = (bsz, lp // tl)

    meta = jnp.broadcast_to(meta_tokens.astype(x.dtype)[None], (bsz, N_META, d))
    hcur = jnp.concatenate([jnp.zeros((bsz, pad, d), x.dtype), meta, x], axis=1)

    o_z = 3 * GDN_WIDTH
    o_b = o_z + GDN_WIDTH
    o_a = o_b + GDN_HEADS
    o_u = o_a + GDN_HEADS
    w_a = w_in[:, :, 0:o_b].astype(BF16)
    w_ba = jnp.zeros((depth, d, 2 * LANES), F32)
    w_ba = w_ba.at[:, :, 0:GDN_HEADS].set(w_in[:, :, o_b:o_a])
    w_ba = w_ba.at[:, :, LANES:LANES + GDN_HEADS].set(w_in[:, :, o_a:o_u]).astype(BF16)
    w_bc = w_in[:, :, o_u:].astype(BF16)
    w_o = w_out.astype(BF16)
    pw_w = conf_pw_w.astype(BF16)
    lane_pad = ((0, 0), (0, LANES - GDN_HEADS))
    alog_p = jnp.pad(a_log.astype(F32), lane_pad)[:, None, :]
    dtb_p = jnp.pad(dt_bias.astype(F32), lane_pad)[:, None, :]
    row2 = lambda a: a.astype(F32)[:, None, :]
    blk = jnp.arange(HEAD_GROUP * CHUNK) // CHUNK
    m256 = (blk[:, None] == blk[None, :]).astype(BF16)

    wide = lambda: pltpu.VMEM((tl, GDN_WIDTH), F32)
    wide16 = lambda: pltpu.VMEM((tl, GDN_WIDTH), BF16)
    narrow = lambda: pltpu.VMEM((tl, LANES), F32)
    scratch = [
        wide16(),
        pltpu.VMEM((tl + QKV_HIST, 3 * GDN_WIDTH), F32),
        wide(), wide(), wide(), wide(),
        narrow(), narrow(),
        pltpu.VMEM((GDN_HEADS, GDN_HEAD_DIM, GDN_HEAD_DIM), F32),
        narrow(),
        pltpu.VMEM((tl // CHUNK * SUBLANES, LANES), F32),
        narrow(),
        wide16(), wide16(), wide16(), wide16(), wide16(), wide16(), wide16(),
        pltpu.VMEM((tl, GDN_WIDTH // 2), BF16),
        wide(),
        wide16(),
        pltpu.VMEM((tl, 7 * CONF_WIDTH), F32),
        pltpu.VMEM((tl + CONF_HIST, CONF_WIDTH), F32),
        pltpu.VMEM((tl + QKV_HIST, SC_WIDTH), F32),
        pltpu.VMEM((tl, CONF_WIDTH), BF16),
        pltpu.VMEM((tl, D_MIX), BF16),
    ]
    in_specs = [_row_spec(tl, d), _const_spec((1, d)), _const_spec((d, o_b)),
                _const_spec((d, 2 * LANES)), _const_spec((GDN_CONV, 3 * GDN_WIDTH)),
                _const_spec((1, LANES)), _const_spec((1, LANES)), _const_spec((1, LANES)),
                _const_spec((HEAD_GROUP * CHUNK, HEAD_GROUP * CHUNK)),
                _const_spec((d, 7 * CONF_WIDTH)),
                _const_spec((CONF_KERNEL, CONF_WIDTH)), _const_spec((1, CONF_WIDTH)),
                _const_spec((1, CONF_WIDTH)), _const_spec((1, CONF_WIDTH)),
                _const_spec((CONF_WIDTH, CONF_WIDTH)), _const_spec((1, CONF_WIDTH)),
                _const_spec((SC_KERNEL, SC_WIDTH)),
                _const_spec((D_MIX, d)), _const_spec((1, d))]

    def layer_call(final):
        return pl.pallas_call(
            functools.partial(_layer_kernel, tl=tl, pad=pad, final=final),
            grid=grid, in_specs=in_specs, out_specs=_row_spec(tl, d),
            out_shape=jax.ShapeDtypeStruct((bsz, lp, d), F32),
            scratch_shapes=scratch,
            compiler_params=pltpu.CompilerParams(
                dimension_semantics=("arbitrary", "arbitrary"), vmem_limit_bytes=VMEM_LIMIT_BYTES),
            name="hybrid_layer")

    fg = final_norm_g.astype(F32)[None, :]
    for l in range(depth):
        hcur = layer_call(l == depth - 1)(
            hcur, norm_g[l].astype(F32)[None, :], w_a[l], w_ba[l], conv_qkv[l].astype(F32),
            alog_p[l], dtb_p[l], gdn_norm_g[l].astype(F32)[None, :], m256,
            w_bc[l], conf_dw_w[l].astype(F32), row2(conf_dw_b)[l], row2(conf_ln_g)[l],
            row2(conf_ln_b)[l], pw_w[l], row2(conf_pw_b)[l], sc_conv_w[l].astype(F32),
            w_o[l], fg)
    return hcur[:, pad + N_META:]
```

```python
import functools
import math

import jax
import jax.numpy as jnp
from jax import lax
from jax.experimental import pallas as pl
from jax.experimental.pallas import tpu as pltpu

F32 = jnp.float32
BF16 = jnp.bfloat16

D_MODEL = 1024
N_META = 16
CHUNK = 64
NORM_EPS = 1e-6
GDN_HEADS = 8
GDN_HEAD_DIM = 128
GDN_WIDTH = GDN_HEADS * GDN_HEAD_DIM
GDN_CONV = 4
CONF_WIDTH = 512
CONF_KERNEL = 31
SC_WIDTH = 512
SC_KERNEL = 3
D_MIX = GDN_WIDTH + CONF_WIDTH + SC_WIDTH

HEAD_GROUP = 4
SUBLANES = 8
LANES = 128
QKV_HIST = 8
CONF_HIST = 32
VMEM_LIMIT_BYTES = 56 * 1024 * 1024
LOG2E = math.log2(math.e)


def _sigmoid(x):
    return 1.0 / (1.0 + jnp.exp2(x * (-LOG2E)))


def _silu(x):
    return x * _sigmoid(x)


def _dot(a, b):
    return jnp.dot(a, b, preferred_element_type=F32)


def _dot_exact(a, b, dims):
    return lax.dot_general(a, b, (dims, ((), ())), precision=lax.Precision.HIGHEST,
                           preferred_element_type=F32)


def _bcast_heads_64(a, left):
    r = a.shape[0]
    return jnp.concatenate(
        [jnp.where(left, jnp.broadcast_to(a[:, 2 * p:2 * p + 1], (r, LANES)),
                   jnp.broadcast_to(a[:, 2 * p + 1:2 * p + 2], (r, LANES)))
         for p in range(GDN_HEADS // 2)], axis=1)


def _masked_block_diag(x16, mask):
    return jnp.concatenate([x16] * HEAD_GROUP, axis=0) * mask


def _block_diag(blocks):
    zero = jnp.zeros_like(blocks[0])
    n = len(blocks)
    return jnp.concatenate(
        [jnp.concatenate([blocks[i] if i == j else zero for j in range(n)], axis=1)
         for i in range(n)], axis=0)


def _layer_kernel(x_ref, ng_ref, wa_ref, wba_ref, cw_ref, alog_ref, dtb_ref, gng_ref, m256_ref,
                  wbc_ref, dww_ref, dwb_ref, lng_ref, lnb_ref, pww_ref, pwb_ref, scw_ref,
                  wo_ref, fg_ref,
                  out_ref,
                  h_s, qkv_buf, q_s, k_s, v_s, za_s, beta_s, g_s, st_ref, gc_s, gct_s, eg_s,
                  q16_s, k16_s, kb_s, vb_s, kbg_s, qd_s, kd_s, qk_s, u_s, w_s,
                  p_s, y_buf, cc_buf, yn_s, mix_s,
                  *, tl, pad, final):
    t = pl.program_id(1)
    n_blocks = tl // CHUNK
    W = CONF_WIDTH
    U, UG, ZB, CB, CC, CH, ZC = (slice(i * W, (i + 1) * W) for i in range(7))
    mix_b0 = GDN_WIDTH
    mix_c0 = GDN_WIDTH + CONF_WIDTH

    @pl.when(t == 0)
    def _():
        qkv_buf[0:QKV_HIST, :] = jnp.zeros((QKV_HIST, 3 * GDN_WIDTH), F32)
        y_buf[0:CONF_HIST, :] = jnp.zeros((CONF_HIST, W), F32)
        cc_buf[0:QKV_HIST, :] = jnp.zeros((QKV_HIST, W), F32)
        st_ref[...] = jnp.zeros_like(st_ref)

    @pl.when(t > 0)
    def _():
        qkv_buf[0:QKV_HIST, :] = qkv_buf[tl:tl + QKV_HIST, :]
        y_buf[0:CONF_HIST, :] = y_buf[tl:tl + CONF_HIST, :]
        cc_buf[0:QKV_HIST, :] = cc_buf[tl:tl + QKV_HIST, :]

    for rb in range(n_blocks):
        rows = slice(rb * CHUNK, (rb + 1) * CHUNK)
        x = x_ref[rows, :]
        ms = jnp.mean(x * x, axis=-1, keepdims=True)
        h_s[rows, :] = (x * lax.rsqrt(ms + NORM_EPS) * ng_ref[...]).astype(BF16)
    h = h_s[...]


    pw = 4 * LANES

    def project_qkv(gi):
        qkv_buf[QKV_HIST:QKV_HIST + tl, gi * pw:(gi + 1) * pw] = _dot(h, wa_ref[:, gi * pw:(gi + 1) * pw])

    def conv_group(gi):
        for cb in range(gi * pw // LANES, (gi + 1) * pw // LANES):
            cs = slice(cb * LANES, (cb + 1) * LANES)
            for rb in range(n_blocks):
                r0 = rb * CHUNK
                strip = qkv_buf[r0:r0 + QKV_HIST + CHUNK, cs]
                acc = strip * cw_ref[0:1, cs]
                for j in range(1, GDN_CONV):
                    acc = pltpu.roll(acc, 1, axis=0) + strip * cw_ref[j:j + 1, cs]
                y = _silu(acc[QKV_HIST:QKV_HIST + CHUNK])
                if cb < 2 * GDN_HEADS:
                    scale = lax.rsqrt(jnp.sum(y * y, axis=-1, keepdims=True) + NORM_EPS)
                    if cb < GDN_HEADS:
                        scale = scale * (GDN_HEAD_DIM ** -0.5)
                    y = y * scale
                dst = (q_s, k_s, v_s)[cb // GDN_HEADS]
                hc = cb % GDN_HEADS
                dst[r0:r0 + CHUNK, hc * LANES:(hc + 1) * LANES] = y

    off31 = CONF_HIST - (CONF_KERNEL - 1)
    conv_cols = {}

    def conv31_block(rb, cb):
        r0 = rb * CHUNK
        cs = slice(cb * LANES, (cb + 1) * LANES)
        acc = None
        for b in range(SUBLANES):
            span = CHUNK + (SUBLANES if b else 0)
            z = None
            for a in range((CONF_KERNEL + off31 + SUBLANES - 1) // SUBLANES):
                j = SUBLANES * a + b - off31
                if 0 <= j < CONF_KERNEL:
                    term = y_buf[r0 + SUBLANES * a:r0 + SUBLANES * a + span, cs] * dww_ref[j:j + 1, cs]
                    z = term if z is None else z + term
            zs = z[b:b + CHUNK]
            acc = zs if acc is None else acc + zs
        conv_cols[rb, cb] = acc + dwb_ref[:, cs]
        if cb == W // LANES - 1:
            y = jnp.concatenate([conv_cols.pop((rb, i)) for i in range(W // LANES)], axis=1)
            mu = jnp.mean(y, axis=-1, keepdims=True)
            yc = y - mu
            var = jnp.mean(yc * yc, axis=-1, keepdims=True)
            yn = yc * lax.rsqrt(var + NORM_EPS) * lng_ref[...] + lnb_ref[...]
            yn_s[r0:r0 + CHUNK, :] = _silu(yn).astype(BF16)

    def project_bc(cols):
        p_s[:, cols] = _dot(h, wbc_ref[:, cols])

    n_qkv_groups = 3 * GDN_WIDTH // pw
    project_qkv(0)
    for gi in range(n_qkv_groups):
        if gi + 1 < n_qkv_groups:
            project_qkv(gi + 1)
        conv_group(gi)
    ba = _dot(h, wba_ref[...])
    row = lax.broadcasted_iota(jnp.int32, (tl, LANES), 0) + t * tl
    live = row >= pad
    beta_s[...] = jnp.where(live, _sigmoid(ba[:, 0:LANES]), 0.0)
    a_in = ba[:, LANES:2 * LANES] + dtb_ref[...]
    softplus = jnp.maximum(a_in, 0.0) + jnp.log1p(jnp.exp(-jnp.abs(a_in)))
    g_s[...] = jnp.where(live, -jnp.exp(alog_ref[...]) * softplus, 0.0)

    def project_za(n0):
        za_s[:, n0:n0 + pw] = _dot(h, wa_ref[:, 3 * GDN_WIDTH + n0:3 * GDN_WIDTH + n0 + pw])

    def glu():
        project_bc(UG)
        y_buf[CONF_HIST:CONF_HIST + tl, :] = p_s[:, U] * _sigmoid(p_s[:, UG])

    def project_cc():
        project_bc(CH)
        cc_buf[QKV_HIST:QKV_HIST + tl, :] = p_s[:, CC] * p_s[:, CH]

    def mixer_b_out():
        zb = _dot(yn_s[...], pww_ref[...]) + pwb_ref[...]
        mix_s[:, mix_b0:mix_c0] = (zb * _silu(p_s[:, ZB])).astype(BF16)

    def mixer_c_rows(rb):
        r0 = rb * CHUNK
        rows = slice(r0, r0 + CHUNK)
        for cb in range(W // LANES):
            cs = slice(cb * LANES, (cb + 1) * LANES)
            acc = None
            for j in range(SC_KERNEL):
                o = r0 + QKV_HIST - (SC_KERNEL - 1) + j
                term = cc_buf[o:o + CHUNK, cs] * scw_ref[j:j + 1, cs]
                acc = term if acc is None else acc + term
            pc = slice(3 * W + cb * LANES, 3 * W + (cb + 1) * LANES)
            pz = slice(6 * W + cb * LANES, 6 * W + (cb + 1) * LANES)
            yc3 = p_s[rows, pc] * acc * _silu(p_s[rows, pz])
            mix_s[rows, mix_c0 + cb * LANES:mix_c0 + (cb + 1) * LANES] = yc3.astype(BF16)

    out_cols = 2 * LANES

    def out_proj_bc(n0):
        out_ref[:, n0:n0 + out_cols] = x_ref[:, n0:n0 + out_cols] + _dot(
            mix_s[:, mix_b0:D_MIX], wo_ref[mix_b0:D_MIX, n0:n0 + out_cols])

    conv31 = [functools.partial(conv31_block, rb, cb)
              for rb in range(n_blocks) for cb in range(W // LANES)]
    projections = [functools.partial(project_za, n0) for n0 in range(0, GDN_WIDTH, pw)]
    projections += [functools.partial(project_bc, CC), project_cc, functools.partial(project_bc, CB),
                    functools.partial(project_bc, ZC), functools.partial(project_bc, ZB)]
    queue = [functools.partial(project_bc, U), glu]
    per_proj = -(-len(conv31) // len(projections))
    for i, proj in enumerate(projections):
        queue += conv31[i * per_proj:(i + 1) * per_proj] + [proj]
    queue += conv31[len(projections) * per_proj:]
    queue += [mixer_b_out] + [functools.partial(mixer_c_rows, rb) for rb in range(n_blocks)]
    queue += [functools.partial(out_proj_bc, n0) for n0 in range(0, D_MODEL, out_cols)]
    inv_steps = 6
    n_gaps = 4 * n_blocks + inv_steps + 1
    gap_no = [0]

    def fill_gap(last=False):
        i = gap_no[0]
        gap_no[0] += 1
        lo = min(i, n_gaps) * len(queue) // n_gaps
        hi = len(queue) if last else min(i + 1, n_gaps) * len(queue) // n_gaps
        for task in queue[lo:hi]:
            task()

    gw = HEAD_GROUP * LANES
    gc4 = HEAD_GROUP * CHUNK
    n_groups = GDN_HEADS // HEAD_GROUP
    ri = lax.broadcasted_iota(jnp.int32, (CHUNK, CHUNK), 0)
    ci = lax.broadcasted_iota(jnp.int32, (CHUNK, CHUNK), 1)
    tri_lo = (ri >= ci).astype(F32)
    ri2 = lax.broadcasted_iota(jnp.int32, (CHUNK, LANES), 0)
    ci2 = lax.broadcasted_iota(jnp.int32, (CHUNK, LANES), 1)
    tri_up2 = (ri2 <= (ci2 % CHUNK)).astype(F32)
    left = ci2 < CHUNK
    rt = lax.broadcasted_iota(jnp.int32, (CHUNK, gc4), 0)
    lt = lax.broadcasted_iota(jnp.int32, (CHUNK, gc4), 1) % CHUNK
    incl_t = rt >= lt
    strict_t = rt > lt
    eye_t = (rt == lt).astype(F32)
    nt_dims = (((1,), (1,)), ((), ()))
    tn_dims = (((0,), (0,)), ((), ()))
    chunk_rows = [slice(c * CHUNK, (c + 1) * CHUNK) for c in range(n_blocks)]
    heads = range(GDN_HEADS)
    hs = [slice(hd * LANES, (hd + 1) * LANES) for hd in heads]

    def head_blocks(ref, rows, g):
        return [ref[rows, hs[hd]] for hd in range(g * HEAD_GROUP, (g + 1) * HEAD_GROUP)]

    for c, rows in enumerate(chunk_rows):
        g_c = g_s[rows, :]
        gc = _dot_exact(tri_lo, g_c, ((1,), (0,)))
        gct = _dot_exact(g_c, tri_up2, ((0,), (0,)))
        gc_s[rows, :] = gc
        gct_s[c * SUBLANES:(c + 1) * SUBLANES, :] = gct[0:SUBLANES, :]
        eg = jnp.exp(gc)
        eg_s[rows, :] = eg
        edec = jnp.exp(gc[CHUNK - 1:CHUNK, :] - gc)
        beta_c = beta_s[rows, :]
        for hd in heads:
            cs = hs[hd]
            q = q_s[rows, cs]
            k = k_s[rows, cs]
            bcol = beta_c[:, hd:hd + 1]
            ecol = eg[:, hd:hd + 1]
            kb = k * bcol
            q16_s[rows, cs] = q.astype(BF16)
            k16_s[rows, cs] = k.astype(BF16)
            kb_s[rows, cs] = kb.astype(BF16)
            vb_s[rows, cs] = (v_s[rows, cs] * bcol).astype(BF16)
            kbg_s[rows, cs] = (kb * ecol).astype(BF16)
            qd_s[rows, cs] = (q * ecol).astype(BF16)
            kd_s[rows, cs] = (k * edec[:, hd:hd + 1]).astype(BF16)
        fill_gap()

    chains = [(c, g) for c in range(n_blocks) for g in range(n_groups)]
    tmat, xmat = {}, {}
    for c, rows in enumerate(chunk_rows):
        cexp = _bcast_heads_64(gc_s[rows, :], left)
        gt = gct_s[c * SUBLANES:(c + 1) * SUBLANES, :]
        rrow = jnp.concatenate(
            [jnp.where(left[0:1], gt[2 * p:2 * p + 1], gt[2 * p + 1:2 * p + 2])
             for p in range(GDN_HEADS // 2)], axis=1)
        dec_all = jnp.exp(cexp - rrow)
        for g in range(n_groups):
            cg = slice(g * gw, (g + 1) * gw)
            cl = slice(g * gc4, (g + 1) * gc4)
            lhs = jnp.concatenate([q16_s[rows, cg], kb_s[rows, cg]], axis=0)
            qkkk = lax.dot_general(lhs, _block_diag(head_blocks(k16_s, rows, g)), nt_dims,
                                   preferred_element_type=F32)
            dec = jnp.where(incl_t, dec_all[:, cl], 0.0)
            qk_s[rows, cl] = (qkkk[0:CHUNK] * dec).astype(BF16)
            nmat = jnp.where(strict_t, qkkk[CHUNK:2 * CHUNK] * dec, 0.0)
            tmat[c, g] = eye_t - nmat
            xmat[c, g] = nmat
        fill_gap()

    for key in chains:
        x16 = xmat[key].astype(BF16)
        xmat[key] = _dot(x16, _masked_block_diag(x16, m256_ref[...]))
    fill_gap()
    for step in range(1, inv_steps - 1):
        for key in chains:
            x16 = xmat[key].astype(BF16)
            r = _dot(jnp.concatenate([tmat[key].astype(BF16), x16], axis=0),
                     _masked_block_diag(x16, m256_ref[...]))
            tmat[key] = tmat[key] + r[0:CHUNK]
            xmat[key] = r[CHUNK:2 * CHUNK]
        fill_gap()
    for key in chains:
        x16 = xmat[key].astype(BF16)
        tmat[key] = tmat[key] + _dot(tmat[key].astype(BF16), _masked_block_diag(x16, m256_ref[...]))
    fill_gap()

    for c, g in chains:
        rows = chunk_rows[c]
        cg = slice(g * gw, (g + 1) * gw)
        t16 = tmat[c, g].astype(BF16)
        u_s[rows, cg] = _dot(t16, _block_diag(head_blocks(vb_s, rows, g)))
        w_s[rows, cg] = _dot(t16, _block_diag(head_blocks(kbg_s, rows, g))).astype(BF16)
    fill_gap()

    pairs = range(GDN_HEADS // 2)
    half = [slice(0, LANES), slice(LANES, 2 * LANES)]
    for c, rows in enumerate(chunk_rows):
        egl = eg_s[c * CHUNK + CHUNK - 1:(c + 1) * CHUNK, :]
        wq = [_dot(jnp.concatenate([w_s[rows, 2 * p * LANES:(2 * p + 2) * LANES],
                                    qd_s[rows, 2 * p * LANES:(2 * p + 2) * LANES]], axis=0),
                   _block_diag([st_ref[2 * p].astype(BF16), st_ref[2 * p + 1].astype(BF16)]))
              for p in pairs]
        fill_gap()
        vn16 = [(u_s[rows, hs[hd]] - wq[hd // 2][0:CHUNK, half[hd % 2]]).astype(BF16) for hd in heads]
        og = [_dot(qk_s[rows, p * LANES:(p + 1) * LANES], _block_diag([vn16[2 * p], vn16[2 * p + 1]]))
              for p in pairs]
        for hd in heads:
            st_ref[hd] = st_ref[hd] * egl[:, hd:hd + 1] + lax.dot_general(
                kd_s[rows, hs[hd]], vn16[hd], tn_dims, preferred_element_type=F32)
        fill_gap(last=(c == n_blocks - 1))
        for hd in heads:
            o = wq[hd // 2][CHUNK:2 * CHUNK, half[hd % 2]] + og[hd // 2][:, half[hd % 2]]
            o = o * lax.rsqrt(jnp.mean(o * o, axis=-1, keepdims=True) + NORM_EPS)
            o = o * gng_ref[...] * _silu(za_s[rows, hs[hd]])
            mix_s[rows, hs[hd]] = o.astype(BF16)

    y = out_ref[...] + _dot(mix_s[:, 0:GDN_WIDTH], wo_ref[0:GDN_WIDTH, :])
    if final:
        y = y * lax.rsqrt(jnp.mean(y * y, axis=-1, keepdims=True) + NORM_EPS) * fg_ref[...]
    out_ref[...] = jnp.where(live[:, 0:1], y, 0.0)


def _const_spec(shape):
    nd = len(shape)
    return pl.BlockSpec(shape, lambda b, t: (0,) * nd, pipeline_mode=pl.Buffered(1))


def _row_spec(tl, width):
    return pl.BlockSpec((None, tl, width), lambda b, t: (b, t, 0))


def _pick_tile(lp):
    for tl in (192, 128, 64):
        if lp % tl == 0:
            return tl
    raise ValueError(f"padded sequence length {lp} is not a multiple of {CHUNK}")


def kernel(x, meta_tokens, norm_g, w_in, conv_qkv, a_log, dt_bias, gdn_norm_g, conf_dw_w, conf_dw_b, conf_ln_g, conf_ln_b, conf_pw_w, conf_pw_b, sc_conv_w, w_out, final_norm_g):
    bsz, seq, d = x.shape
    depth = w_in.shape[0]
    assert d == D_MODEL
    length = N_META + seq
    pad = (-length) % CHUNK
    lp = length + pad
    tl = _pick_tile(lp)
    grid = (bsz, lp // tl)

    meta = jnp.broadcast_to(meta_tokens.astype(x.dtype)[None], (bsz, N_META, d))
    hcur = jnp.concatenate([jnp.zeros((bsz, pad, d), x.dtype), meta, x], axis=1)

    o_z = 3 * GDN_WIDTH
    o_b = o_z + GDN_WIDTH
    o_a = o_b + GDN_HEADS
    o_u = o_a + GDN_HEADS
    w_a = w_in[:, :, 0:o_b].astype(BF16)
    w_ba = jnp.zeros((depth, d, 2 * LANES), F32)
    w_ba = w_ba.at[:, :, 0:GDN_HEADS].set(w_in[:, :, o_b:o_a])
    w_ba = w_ba.at[:, :, LANES:LANES + GDN_HEADS].set(w_in[:, :, o_a:o_u]).astype(BF16)
    w_bc = w_in[:, :, o_u:].astype(BF16)
    w_o = w_out.astype(BF16)
    pw_w = conf_pw_w.astype(BF16)
    lane_pad = ((0, 0), (0, LANES - GDN_HEADS))
    alog_p = jnp.pad(a_log.astype(F32), lane_pad)[:, None, :]
    dtb_p = jnp.pad(dt_bias.astype(F32), lane_pad)[:, None, :]
    row2 = lambda a: a.astype(F32)[:, None, :]
    blk = jnp.arange(HEAD_GROUP * CHUNK) // CHUNK
    m256 = (blk[:, None] == blk[None, :]).astype(BF16)

    wide = lambda: pltpu.VMEM((tl, GDN_WIDTH), F32)
    wide16 = lambda: pltpu.VMEM((tl, GDN_WIDTH), BF16)
    narrow = lambda: pltpu.VMEM((tl, LANES), F32)
    scratch = [
        wide16(),
        pltpu.VMEM((tl + QKV_HIST, 3 * GDN_WIDTH), F32),
        wide(), wide(), wide(), wide(),
        narrow(), narrow(),
        pltpu.VMEM((GDN_HEADS, GDN_HEAD_DIM, GDN_HEAD_DIM), F32),
        narrow(),
        pltpu.VMEM((tl // CHUNK * SUBLANES, LANES), F32),
        narrow(),
        wide16(), wide16(), wide16(), wide16(), wide16(), wide16(), wide16(),
        pltpu.VMEM((tl, GDN_WIDTH // 2), BF16),
        wide(),
        wide16(),
        pltpu.VMEM((tl, 7 * CONF_WIDTH), F32),
        pltpu.VMEM((tl + CONF_HIST, CONF_WIDTH), F32),
        pltpu.VMEM((tl + QKV_HIST, SC_WIDTH), F32),
        pltpu.VMEM((tl, CONF_WIDTH), BF16),
        pltpu.VMEM((tl, D_MIX), BF16),
    ]
    in_specs = [_row_spec(tl, d), _const_spec((1, d)), _const_spec((d, o_b)),
                _const_spec((d, 2 * LANES)), _const_spec((GDN_CONV, 3 * GDN_WIDTH)),
                _const_spec((1, LANES)), _const_spec((1, LANES)), _const_spec((1, LANES)),
                _const_spec((HEAD_GROUP * CHUNK, HEAD_GROUP * CHUNK)),
                _const_spec((d, 7 * CONF_WIDTH)),
                _const_spec((CONF_KERNEL, CONF_WIDTH)), _const_spec((1, CONF_WIDTH)),
                _const_spec((1, CONF_WIDTH)), _const_spec((1, CONF_WIDTH)),
                _const_spec((CONF_WIDTH, CONF_WIDTH)), _const_spec((1, CONF_WIDTH)),
                _const_spec((SC_KERNEL, SC_WIDTH)),
                _const_spec((D_MIX, d)), _const_spec((1, d))]

    def layer_call(final):
        return pl.pallas_call(
            functools.partial(_layer_kernel, tl=tl, pad=pad, final=final),
            grid=grid, in_specs=in_specs, out_specs=_row_spec(tl, d),
            out_shape=jax.ShapeDtypeStruct((bsz, lp, d), F32),
            scratch_shapes=scratch,
            compiler_params=pltpu.CompilerParams(
                dimension_semantics=("arbitrary", "arbitrary"), vmem_limit_bytes=VMEM_LIMIT_BYTES),
            name="hybrid_layer")

    fg = final_norm_g.astype(F32)[None, :]
    for l in range(depth):
        hcur = layer_call(l == depth - 1)(
            hcur, norm_g[l].astype(F32)[None, :], w_a[l], w_ba[l], conv_qkv[l].astype(F32),
            alog_p[l], dtb_p[l], gdn_norm_g[l].astype(F32)[None, :], m256,
            w_bc[l], conf_dw_w[l].astype(F32), row2(conf_dw_b)[l], row2(conf_ln_g)[l],
            row2(conf_ln_b)[l], pw_w[l], row2(conf_pw_b)[l], sc_conv_w[l].astype(F32),
            w_o[l], fg)
    return hcur[:, pad + N_META:]
```

```python
import functools
import math

import jax
import jax.numpy as jnp
from jax import lax
from jax.experimental import pallas as pl
from jax.experimental.pallas import tpu as pltpu

F32 = jnp.float32
BF16 = jnp.bfloat16

D_MODEL = 1024
N_META = 16
CHUNK = 64
NORM_EPS = 1e-6
GDN_HEADS = 8
GDN_HEAD_DIM = 128
GDN_WIDTH = GDN_HEADS * GDN_HEAD_DIM
GDN_CONV = 4
CONF_WIDTH = 512
CONF_KERNEL = 31
SC_WIDTH = 512
SC_KERNEL = 3
D_MIX = GDN_WIDTH + CONF_WIDTH + SC_WIDTH

HEAD_GROUP = 4
SUBLANES = 8
LANES = 128
QKV_HIST = 8
CONF_HIST = 32
VMEM_LIMIT_BYTES = 56 * 1024 * 1024
LOG2E = math.log2(math.e)


def _sigmoid(x):
    return 1.0 / (1.0 + jnp.exp2(x * (-LOG2E)))


def _silu(x):
    return x * _sigmoid(x)


def _dot(a, b):
    return jnp.dot(a, b, preferred_element_type=F32)


def _bcast_heads_64(a, left):
    r = a.shape[0]
    return jnp.concatenate(
        [jnp.where(left, jnp.broadcast_to(a[:, 2 * p:2 * p + 1], (r, LANES)),
                   jnp.broadcast_to(a[:, 2 * p + 1:2 * p + 2], (r, LANES)))
         for p in range(GDN_HEADS // 2)], axis=1)


def _masked_block_diag(x16, mask):
    return jnp.concatenate([x16] * HEAD_GROUP, axis=0) * mask


def _block_diag(blocks):
    zero = jnp.zeros_like(blocks[0])
    n = len(blocks)
    return jnp.concatenate(
        [jnp.concatenate([blocks[i] if i == j else zero for j in range(n)], axis=1)
         for i in range(n)], axis=0)


def _layer_kernel(*refs, tl, pad, first, final):
    n_blocks = tl // CHUNK
    refs = list(refs)
    x_blocks = [refs.pop(0) for _ in range(n_blocks)] if first else None
    x_ref = refs.pop(0)
    (ng_ref, wa_ref, wba_ref, cw_ref, alog_ref, dtb_ref, gng_ref, m256_ref,
     wbc_ref, dww_ref, dwb_ref, lng_ref, lnb_ref, pww_ref, pwb_ref, scw_ref,
     wo_ref, fg_ref, out_ref,
     h_s, qkv_buf, q_s, k_s, v_s, za_s, beta_s, g_s, st_ref, gc_s, gct_s, eg_s,
     q16_s, k16_s, kb_s, vb_s, kbg_s, qd_s, kd_s, qk_s, u_s, w_s,
     p_s, y_buf, cc_buf, yn_s, mix_s) = refs[:46]
    t = pl.program_id(1)
    W = CONF_WIDTH

    if first:
        xt_s = refs[46]
        for k, blk in enumerate(x_blocks):
            v = blk[...]
            if k == 0:
                v = jnp.where(t == 0, x_ref[...], v)
            xt_s[k * CHUNK:(k + 1) * CHUNK, :] = v
        x_ref = xt_s

    U, UG, ZB, CB, CC, CH, ZC = (slice(i * W, (i + 1) * W) for i in range(7))
    mix_b0 = GDN_WIDTH
    mix_c0 = GDN_WIDTH + CONF_WIDTH

    @pl.when(t == 0)
    def _():
        qkv_buf[0:QKV_HIST, :] = jnp.zeros((QKV_HIST, 3 * GDN_WIDTH), F32)
        y_buf[0:CONF_HIST, :] = jnp.zeros((CONF_HIST, W), F32)
        cc_buf[0:QKV_HIST, :] = jnp.zeros((QKV_HIST, W), F32)
        st_ref[...] = jnp.zeros_like(st_ref)

    @pl.when(t > 0)
    def _():
        qkv_buf[0:QKV_HIST, :] = qkv_buf[tl:tl + QKV_HIST, :]
        y_buf[0:CONF_HIST, :] = y_buf[tl:tl + CONF_HIST, :]
        cc_buf[0:QKV_HIST, :] = cc_buf[tl:tl + QKV_HIST, :]

    for rb in range(n_blocks):
        rows = slice(rb * CHUNK, (rb + 1) * CHUNK)
        x = x_ref[rows, :]
        ms = jnp.mean(x * x, axis=-1, keepdims=True)
        h_s[rows, :] = (x * lax.rsqrt(ms + NORM_EPS) * ng_ref[...]).astype(BF16)
    h = h_s[...]


    pw = 4 * LANES

    def project_qkv(gi):
        qkv_buf[QKV_HIST:QKV_HIST + tl, gi * pw:(gi + 1) * pw] = _dot(h, wa_ref[:, gi * pw:(gi + 1) * pw])

    def conv_group(gi):
        for cb in range(gi * pw // LANES, (gi + 1) * pw // LANES):
            cs = slice(cb * LANES, (cb + 1) * LANES)
            for rb in range(n_blocks):
                r0 = rb * CHUNK
                strip = qkv_buf[r0:r0 + QKV_HIST + CHUNK, cs]
                acc = strip * cw_ref[0:1, cs]
                for j in range(1, GDN_CONV):
                    acc = pltpu.roll(acc, 1, axis=0) + strip * cw_ref[j:j + 1, cs]
                y = _silu(acc[QKV_HIST:QKV_HIST + CHUNK])
                if cb < 2 * GDN_HEADS:
                    scale = lax.rsqrt(jnp.sum(y * y, axis=-1, keepdims=True) + NORM_EPS)
                    if cb < GDN_HEADS:
                        scale = scale * (GDN_HEAD_DIM ** -0.5)
                    y = y * scale
                dst = (q_s, k_s, v_s)[cb // GDN_HEADS]
                hc = cb % GDN_HEADS
                dst[r0:r0 + CHUNK, hc * LANES:(hc + 1) * LANES] = y

    off31 = CONF_HIST - (CONF_KERNEL - 1)
    conv_cols = {}

    def conv31_block(rb, cb):
        r0 = rb * CHUNK
        cs = slice(cb * LANES, (cb + 1) * LANES)
        acc = None
        for b in range(SUBLANES):
            span = CHUNK + (SUBLANES if b else 0)
            z = None
            for a in range((CONF_KERNEL + off31 + SUBLANES - 1) // SUBLANES):
                j = SUBLANES * a + b - off31
                if 0 <= j < CONF_KERNEL:
                    term = y_buf[r0 + SUBLANES * a:r0 + SUBLANES * a + span, cs] * dww_ref[j:j + 1, cs]
                    z = term if z is None else z + term
            zs = z[b:b + CHUNK]
            acc = zs if acc is None else acc + zs
        conv_cols[rb, cb] = acc + dwb_ref[:, cs]
        if cb == W // LANES - 1:
            y = jnp.concatenate([conv_cols.pop((rb, i)) for i in range(W // LANES)], axis=1)
            mu = jnp.mean(y, axis=-1, keepdims=True)
            yc = y - mu
            var = jnp.mean(yc * yc, axis=-1, keepdims=True)
            yn = yc * lax.rsqrt(var + NORM_EPS) * lng_ref[...] + lnb_ref[...]
            yn_s[r0:r0 + CHUNK, :] = _silu(yn).astype(BF16)

    def project_bc(cols):
        p_s[:, cols] = _dot(h, wbc_ref[:, cols])

    n_qkv_groups = 3 * GDN_WIDTH // pw
    project_qkv(0)
    for gi in range(n_qkv_groups):
        if gi + 1 < n_qkv_groups:
            project_qkv(gi + 1)
        conv_group(gi)
    ba = _dot(h, wba_ref[...])
    row = lax.broadcasted_iota(jnp.int32, (tl, LANES), 0) + t * tl
    live = row >= pad
    beta_s[...] = jnp.where(live, _sigmoid(ba), 0.0)
    a_in = pltpu.roll(ba, LANES - GDN_HEADS, axis=1) + dtb_ref[...]
    softplus = jnp.maximum(a_in, 0.0) + jnp.log1p(jnp.exp(-jnp.abs(a_in)))
    g_s[...] = jnp.where(live, -jnp.exp(alog_ref[...]) * softplus, 0.0)

    def project_za(n0):
        za_s[:, n0:n0 + pw] = _dot(h, wa_ref[:, 3 * GDN_WIDTH + n0:3 * GDN_WIDTH + n0 + pw])

    def glu():
        project_bc(UG)
        y_buf[CONF_HIST:CONF_HIST + tl, :] = p_s[:, U] * _sigmoid(p_s[:, UG])

    def project_cc():
        project_bc(CH)
        cc_buf[QKV_HIST:QKV_HIST + tl, :] = p_s[:, CC] * p_s[:, CH]

    def mixer_b_out():
        zb = _dot(yn_s[...], pww_ref[...]) + pwb_ref[...]
        mix_s[:, mix_b0:mix_c0] = (zb * _silu(p_s[:, ZB])).astype(BF16)

    def mixer_c_rows(rb):
        r0 = rb * CHUNK
        rows = slice(r0, r0 + CHUNK)
        for cb in range(W // LANES):
            cs = slice(cb * LANES, (cb + 1) * LANES)
            acc = None
            for j in range(SC_KERNEL):
                o = r0 + QKV_HIST - (SC_KERNEL - 1) + j
                term = cc_buf[o:o + CHUNK, cs] * scw_ref[j:j + 1, cs]
                acc = term if acc is None else acc + term
            pc = slice(3 * W + cb * LANES, 3 * W + (cb + 1) * LANES)
            pz = slice(6 * W + cb * LANES, 6 * W + (cb + 1) * LANES)
            yc3 = p_s[rows, pc] * acc * _silu(p_s[rows, pz])
            mix_s[rows, mix_c0 + cb * LANES:mix_c0 + (cb + 1) * LANES] = yc3.astype(BF16)

    out_cols = 2 * LANES

    def out_proj_bc(n0):
        out_ref[:, n0:n0 + out_cols] = x_ref[:, n0:n0 + out_cols] + _dot(
            mix_s[:, mix_b0:D_MIX], wo_ref[mix_b0:D_MIX, n0:n0 + out_cols])

    conv31 = [functools.partial(conv31_block, rb, cb)
              for rb in range(n_blocks) for cb in range(W // LANES)]
    projections = [functools.partial(project_za, n0) for n0 in range(0, GDN_WIDTH, pw)]
    projections += [functools.partial(project_bc, CC), project_cc, functools.partial(project_bc, CB),
                    functools.partial(project_bc, ZC), functools.partial(project_bc, ZB)]
    queue = [functools.partial(project_bc, U), glu]
    per_proj = -(-len(conv31) // len(projections))
    for i, proj in enumerate(projections):
        queue += conv31[i * per_proj:(i + 1) * per_proj] + [proj]
    queue += conv31[len(projections) * per_proj:]
    queue += [mixer_b_out] + [functools.partial(mixer_c_rows, rb) for rb in range(n_blocks)]
    queue += [functools.partial(out_proj_bc, n0) for n0 in range(0, D_MODEL, out_cols)]
    inv_steps = 6
    n_gaps = 4 * n_blocks + inv_steps + 1
    gap_no = [0]

    def fill_gap(last=False):
        i = gap_no[0]
        gap_no[0] += 1
        lo = min(i, n_gaps) * len(queue) // n_gaps
        hi = len(queue) if last else min(i + 1, n_gaps) * len(queue) // n_gaps
        for task in queue[lo:hi]:
            task()

    gw = HEAD_GROUP * LANES
    gc4 = HEAD_GROUP * CHUNK
    n_groups = GDN_HEADS // HEAD_GROUP
    ri = lax.broadcasted_iota(jnp.int32, (CHUNK, CHUNK), 0)
    ci = lax.broadcasted_iota(jnp.int32, (CHUNK, CHUNK), 1)
    tri_lo = (ri >= ci).astype(BF16)
    ri2 = lax.broadcasted_iota(jnp.int32, (CHUNK, LANES), 0)
    ci2 = lax.broadcasted_iota(jnp.int32, (CHUNK, LANES), 1)
    tri_up2 = (ri2 <= (ci2 % CHUNK)).astype(BF16)
    left = ci2 < CHUNK
    rt = lax.broadcasted_iota(jnp.int32, (CHUNK, gc4), 0)
    lt = lax.broadcasted_iota(jnp.int32, (CHUNK, gc4), 1) % CHUNK
    incl_t = rt >= lt
    strict_t = rt > lt
    eye_t = (rt == lt).astype(F32)
    nt_dims = (((1,), (1,)), ((), ()))
    tn_dims = (((0,), (0,)), ((), ()))
    chunk_rows = [slice(c * CHUNK, (c + 1) * CHUNK) for c in range(n_blocks)]
    heads = range(GDN_HEADS)
    hs = [slice(hd * LANES, (hd + 1) * LANES) for hd in heads]

    def head_blocks(ref, rows, g):
        return [ref[rows, hs[hd]] for hd in range(g * HEAD_GROUP, (g + 1) * HEAD_GROUP)]

    for c, rows in enumerate(chunk_rows):
        g_c = g_s[rows, :]
        g_hi = g_c.astype(BF16)
        g_r1 = g_c - g_hi.astype(F32)
        g_mid = g_r1.astype(BF16)
        g_lo = (g_r1 - g_mid.astype(F32)).astype(BF16)
        gc = _dot(tri_lo, g_hi) + _dot(tri_lo, g_mid) + _dot(tri_lo, g_lo)
        gct = sum(lax.dot_general(part, tri_up2, tn_dims, preferred_element_type=F32)
                  for part in (g_hi, g_mid, g_lo))
        gc_s[rows, :] = gc
        gct_s[c * SUBLANES:(c + 1) * SUBLANES, :] = gct[0:SUBLANES, :]
        eg = jnp.exp(gc)
        eg_s[rows, :] = eg
        edec = jnp.exp(gc[CHUNK - 1:CHUNK, :] - gc)
        beta_c = beta_s[rows, :]
        for hd in heads:
            cs = hs[hd]
            q = q_s[rows, cs]
            k = k_s[rows, cs]
            bcol = beta_c[:, hd:hd + 1]
            ecol = eg[:, hd:hd + 1]
            kb = k * bcol
            q16_s[rows, cs] = q.astype(BF16)
            k16_s[rows, cs] = k.astype(BF16)
            kb_s[rows, cs] = kb.astype(BF16)
            vb_s[rows, cs] = (v_s[rows, cs] * bcol).astype(BF16)
            kbg_s[rows, cs] = (kb * ecol).astype(BF16)
            qd_s[rows, cs] = (q * ecol).astype(BF16)
            kd_s[rows, cs] = (k * edec[:, hd:hd + 1]).astype(BF16)
        fill_gap()

    chains = [(c, g) for c in range(n_blocks) for g in range(n_groups)]
    tmat, xmat = {}, {}
    for c, rows in enumerate(chunk_rows):
        cexp = _bcast_heads_64(gc_s[rows, :], left)
        gt = gct_s[c * SUBLANES:(c + 1) * SUBLANES, :]
        rrow = jnp.concatenate(
            [jnp.where(left[0:1], gt[2 * p:2 * p + 1], gt[2 * p + 1:2 * p + 2])
             for p in range(GDN_HEADS // 2)], axis=1)
        dec_all = jnp.exp(cexp - rrow)
        for g in range(n_groups):
            cg = slice(g * gw, (g + 1) * gw)
            cl = slice(g * gc4, (g + 1) * gc4)
            lhs = jnp.concatenate([q16_s[rows, cg], kb_s[rows, cg]], axis=0)
            qkkk = lax.dot_general(lhs, _block_diag(head_blocks(k16_s, rows, g)), nt_dims,
                                   preferred_element_type=F32)
            dec = jnp.where(incl_t, dec_all[:, cl], 0.0)
            qk_s[rows, cl] = (qkkk[0:CHUNK] * dec).astype(BF16)
            nmat = jnp.where(strict_t, qkkk[CHUNK:2 * CHUNK] * dec, 0.0)
            tmat[c, g] = eye_t - nmat
            xmat[c, g] = nmat
        fill_gap()

    for key in chains:
        x16 = xmat[key].astype(BF16)
        xmat[key] = _dot(x16, _masked_block_diag(x16, m256_ref[...]))
    fill_gap()
    for step in range(1, inv_steps - 1):
        for key in chains:
            x16 = xmat[key].astype(BF16)
            r = _dot(jnp.concatenate([tmat[key].astype(BF16), x16], axis=0),
                     _masked_block_diag(x16, m256_ref[...]))
            tmat[key] = tmat[key] + r[0:CHUNK]
            xmat[key] = r[CHUNK:2 * CHUNK]
        fill_gap()
    for key in chains:
        x16 = xmat[key].astype(BF16)
        tmat[key] = tmat[key] + _dot(tmat[key].astype(BF16), _masked_block_diag(x16, m256_ref[...]))
    fill_gap()

    for c, g in chains:
        rows = chunk_rows[c]
        cg = slice(g * gw, (g + 1) * gw)
        t16 = tmat[c, g].astype(BF16)
        u_s[rows, cg] = _dot(t16, _block_diag(head_blocks(vb_s, rows, g)))
        w_s[rows, cg] = _dot(t16, _block_diag(head_blocks(kbg_s, rows, g))).astype(BF16)
    fill_gap()

    pairs = range(GDN_HEADS // 2)
    half = [slice(0, LANES), slice(LANES, 2 * LANES)]
    for c, rows in enumerate(chunk_rows):
        egl = eg_s[c * CHUNK + CHUNK - 1:(c + 1) * CHUNK, :]
        wq = [_dot(jnp.concatenate([w_s[rows, 2 * p * LANES:(2 * p + 2) * LANES],
                                    qd_s[rows, 2 * p * LANES:(2 * p + 2) * LANES]], axis=0),
                   _block_diag([st_ref[2 * p].astype(BF16), st_ref[2 * p + 1].astype(BF16)]))
              for p in pairs]
        fill_gap()
        vn16 = [(u_s[rows, hs[hd]] - wq[hd // 2][0:CHUNK, half[hd % 2]]).astype(BF16) for hd in heads]
        og = [_dot(qk_s[rows, p * LANES:(p + 1) * LANES], _block_diag([vn16[2 * p], vn16[2 * p + 1]]))
              for p in pairs]
        for hd in heads:
            st_ref[hd] = st_ref[hd] * egl[:, hd:hd + 1] + lax.dot_general(
                kd_s[rows, hs[hd]], vn16[hd], tn_dims, preferred_element_type=F32)
        fill_gap(last=(c == n_blocks - 1))
        for hd in heads:
            o = wq[hd // 2][CHUNK:2 * CHUNK, half[hd % 2]] + og[hd // 2][:, half[hd % 2]]
            o = o * lax.rsqrt(jnp.mean(o * o, axis=-1, keepdims=True) + NORM_EPS)
            o = o * gng_ref[...] * _silu(za_s[rows, hs[hd]])
            mix_s[rows, hs[hd]] = o.astype(BF16)

    y = out_ref[...] + _dot(mix_s[:, 0:GDN_WIDTH], wo_ref[0:GDN_WIDTH, :])
    if final:
        y = y * lax.rsqrt(jnp.mean(y * y, axis=-1, keepdims=True) + NORM_EPS) * fg_ref[...]
    out_ref[...] = jnp.where(live[:, 0:1], y, 0.0)


def _const_spec(shape):
    nd = len(shape)
    return pl.BlockSpec(shape, lambda b, t: (0,) * nd, pipeline_mode=pl.Buffered(1))


def _layer_spec(shape, layer, col_block=0):
    nd = len(shape)
    return pl.BlockSpec((None,) + tuple(shape),
                        lambda b, t: (layer,) + (0,) * (nd - 1) + (col_block,),
                        pipeline_mode=pl.Buffered(1))


def _row_spec(tl, width):
    return pl.BlockSpec((None, tl, width), lambda b, t: (b, t, 0))


def _pick_tile(lp):
    for tl in (192, 128, 64):
        if lp % tl == 0:
            return tl
    raise ValueError(f"padded sequence length {lp} is not a multiple of {CHUNK}")


def kernel(x, meta_tokens, norm_g, w_in, conv_qkv, a_log, dt_bias, gdn_norm_g, conf_dw_w, conf_dw_b, conf_ln_g, conf_ln_b, conf_pw_w, conf_pw_b, sc_conv_w, w_out, final_norm_g):
    bsz, seq, d = x.shape
    depth = w_in.shape[0]
    assert d == D_MODEL
    length = N_META + seq
    pad = (-length) % CHUNK
    front = pad + N_META
    assert front == CHUNK, "the meta tokens and their padding must fill exactly one chunk"
    lp = length + pad
    tl = _pick_tile(lp)
    n_blocks = tl // CHUNK
    grid = (bsz, lp // tl)

    front_rows = jnp.concatenate([jnp.zeros((pad, d), F32), meta_tokens.astype(F32)], axis=0)

    o_b = 3 * GDN_WIDTH + GDN_WIDTH
    o_u = o_b + 2 * GDN_HEADS
    assert o_b % LANES == 0
    w16 = w_in.astype(BF16)
    w_bc = w16[:, :, o_u:]
    w_o = w_out.astype(BF16)
    pw_w = conf_pw_w.astype(BF16)
    lane_pad = ((0, 0), (0, LANES - GDN_HEADS))
    row2 = lambda a: a.astype(F32)[:, None, :]
    alog_p = row2(jnp.pad(a_log, lane_pad))
    dtb_p = row2(jnp.pad(dt_bias, lane_pad))
    blk = jnp.arange(HEAD_GROUP * CHUNK) // CHUNK
    m256 = (blk[:, None] == blk[None, :]).astype(BF16)

    wide = lambda: pltpu.VMEM((tl, GDN_WIDTH), F32)
    wide16 = lambda: pltpu.VMEM((tl, GDN_WIDTH), BF16)
    narrow = lambda: pltpu.VMEM((tl, LANES), F32)
    scratch = [
        wide16(),
        pltpu.VMEM((tl + QKV_HIST, 3 * GDN_WIDTH), F32),
        wide(), wide(), wide(), wide(),
        narrow(), narrow(),
        pltpu.VMEM((GDN_HEADS, GDN_HEAD_DIM, GDN_HEAD_DIM), F32),
        narrow(),
        pltpu.VMEM((n_blocks * SUBLANES, LANES), F32),
        narrow(),
        wide16(), wide16(), wide16(), wide16(), wide16(), wide16(), wide16(),
        pltpu.VMEM((tl, GDN_WIDTH // 2), BF16),
        wide(),
        wide16(),
        pltpu.VMEM((tl, 7 * CONF_WIDTH), F32),
        pltpu.VMEM((tl + CONF_HIST, CONF_WIDTH), F32),
        pltpu.VMEM((tl + QKV_HIST, SC_WIDTH), F32),
        pltpu.VMEM((tl, CONF_WIDTH), BF16),
        pltpu.VMEM((tl, D_MIX), BF16),
    ]

    def x_block_spec(k):
        return pl.BlockSpec((None, CHUNK, d),
                            lambda b, t: (b, jnp.maximum(n_blocks * t + k - front // CHUNK, 0), 0))

    def layer_call(layer, first, final):
        if first:
            x_specs = [x_block_spec(k) for k in range(n_blocks)] + [_const_spec((CHUNK, d))]
        else:
            x_specs = [_row_spec(tl, d)]
        param_specs = [
            _layer_spec((1, d), layer),
            _layer_spec((d, o_b), layer),
            _layer_spec((d, LANES), layer, col_block=o_b // LANES),
            _layer_spec((GDN_CONV, 3 * GDN_WIDTH), layer),
            _layer_spec((1, LANES), layer), _layer_spec((1, LANES), layer),
            _layer_spec((1, LANES), layer),
            _const_spec((HEAD_GROUP * CHUNK, HEAD_GROUP * CHUNK)),
            _layer_spec((d, 7 * CONF_WIDTH), layer),
            _layer_spec((CONF_KERNEL, CONF_WIDTH), layer), _layer_spec((1, CONF_WIDTH), layer),
            _layer_spec((1, CONF_WIDTH), layer), _layer_spec((1, CONF_WIDTH), layer),
            _layer_spec((CONF_WIDTH, CONF_WIDTH), layer), _layer_spec((1, CONF_WIDTH), layer),
            _layer_spec((SC_KERNEL, SC_WIDTH), layer),
            _layer_spec((D_MIX, d), layer), _const_spec((1, d))]
        extra = [pltpu.VMEM((tl, d), F32)] if first else []
        return pl.pallas_call(
            functools.partial(_layer_kernel, tl=tl, pad=pad, first=first, final=final),
            grid=grid, in_specs=x_specs + param_specs, out_specs=_row_spec(tl, d),
            out_shape=jax.ShapeDtypeStruct((bsz, lp, d), F32),
            scratch_shapes=scratch + extra,
            compiler_params=pltpu.CompilerParams(
                dimension_semantics=("arbitrary", "arbitrary"), vmem_limit_bytes=VMEM_LIMIT_BYTES),
            name="hybrid_layer")

    params = (row2(norm_g), w16, w16, conv_qkv.astype(F32), alog_p, dtb_p, row2(gdn_norm_g), m256,
              w_bc, conf_dw_w.astype(F32), row2(conf_dw_b), row2(conf_ln_g), row2(conf_ln_b),
              pw_w, row2(conf_pw_b), sc_conv_w.astype(F32), w_o, final_norm_g.astype(F32)[None, :])
    hcur = None
    for layer in range(depth):
        first, final = layer == 0, layer == depth - 1
        x_args = [x] * n_blocks + [front_rows] if first else [hcur]
        hcur = layer_call(layer, first, final)(*x_args, *params)
    return hcur[:, front:]
```

```python
import functools
import math

import jax
import jax.numpy as jnp
from jax import lax
from jax.experimental import pallas as pl
from jax.experimental.pallas import tpu as pltpu

F32 = jnp.float32
BF16 = jnp.bfloat16

D_MODEL = 1024
N_META = 16
CHUNK = 64
NORM_EPS = 1e-6
GDN_HEADS = 8
GDN_HEAD_DIM = 128
GDN_WIDTH = GDN_HEADS * GDN_HEAD_DIM
GDN_CONV = 4
CONF_WIDTH = 512
CONF_KERNEL = 31
SC_WIDTH = 512
SC_KERNEL = 3
D_MIX = GDN_WIDTH + CONF_WIDTH + SC_WIDTH

HEAD_GROUP = 4
SUBLANES = 8
LANES = 128
QKV_HIST = 8
CONF_HIST = 32
PARAM_ROWS = 8
PARAM_LANES = 512
VMEM_LIMIT_BYTES = 56 * 1024 * 1024
LOG2E = math.log2(math.e)


def _sigmoid(x):
    return 1.0 / (1.0 + jnp.exp2(x * (-LOG2E)))


def _silu(x):
    return x * _sigmoid(x)


def _dot(a, b):
    return jnp.dot(a, b, preferred_element_type=F32)


def _dot_exact(a, b, dims):
    return lax.dot_general(a, b, (dims, ((), ())), precision=lax.Precision.HIGHEST,
                           preferred_element_type=F32)


def _bcast_heads_64(a, left):
    r = a.shape[0]
    return jnp.concatenate(
        [jnp.where(left, jnp.broadcast_to(a[:, 2 * p:2 * p + 1], (r, LANES)),
                   jnp.broadcast_to(a[:, 2 * p + 1:2 * p + 2], (r, LANES)))
         for p in range(GDN_HEADS // 2)], axis=1)


def _masked_block_diag(x16, mask):
    return jnp.concatenate([x16] * HEAD_GROUP, axis=0) * mask


def _block_diag(blocks):
    zero = jnp.zeros_like(blocks[0])
    n = len(blocks)
    return jnp.concatenate(
        [jnp.concatenate([blocks[i] if i == j else zero for j in range(n)], axis=1)
         for i in range(n)], axis=0)


def _layer_kernel(x_ref, ng_ref, wa_ref, wba_ref, cw_ref, alog_ref, dtb_ref, gng_ref, m256_ref,
                  wbc_ref, dww_ref, dwb_ref, lng_ref, lnb_ref, pww_ref, pwb_ref, scw_ref,
                  wo_ref, fg_ref,
                  out_ref,
                  h_s, qkv_buf, q_s, k_s, v_s, za_s, beta_s, g_s, st_ref, gc_s, gct_s, eg_s,
                  q16_s, k16_s, kb_s, vb_s, kbg_s, qd_s, kd_s, qk_s, u_s, w_s,
                  p_s, y_buf, cc_buf, yn_s, mix_s,
                  *, tl, pad, final):
    t = pl.program_id(1)
    n_blocks = tl // CHUNK
    W = CONF_WIDTH
    U, UG, ZB, CB, CC, CH, ZC = (slice(i * W, (i + 1) * W) for i in range(7))
    mix_b0 = GDN_WIDTH
    mix_c0 = GDN_WIDTH + CONF_WIDTH

    @pl.when(t == 0)
    def _():
        qkv_buf[0:QKV_HIST, :] = jnp.zeros((QKV_HIST, 3 * GDN_WIDTH), F32)
        y_buf[0:CONF_HIST, :] = jnp.zeros((CONF_HIST, W), F32)
        cc_buf[0:QKV_HIST, :] = jnp.zeros((QKV_HIST, W), F32)
        st_ref[...] = jnp.zeros_like(st_ref)

    @pl.when(t > 0)
    def _():
        qkv_buf[0:QKV_HIST, :] = qkv_buf[tl:tl + QKV_HIST, :]
        y_buf[0:CONF_HIST, :] = y_buf[tl:tl + CONF_HIST, :]
        cc_buf[0:QKV_HIST, :] = cc_buf[tl:tl + QKV_HIST, :]

    for rb in range(n_blocks):
        rows = slice(rb * CHUNK, (rb + 1) * CHUNK)
        x = x_ref[rows, :]
        ms = jnp.mean(x * x, axis=-1, keepdims=True)
        h_s[rows, :] = (x * lax.rsqrt(ms + NORM_EPS) * ng_ref[0:1, :]).astype(BF16)
    h = h_s[...]


    pw = 4 * LANES

    def project_qkv(gi):
        qkv_buf[QKV_HIST:QKV_HIST + tl, gi * pw:(gi + 1) * pw] = _dot(h, wa_ref[:, gi * pw:(gi + 1) * pw])

    def conv_group(gi):
        for cb in range(gi * pw // LANES, (gi + 1) * pw // LANES):
            cs = slice(cb * LANES, (cb + 1) * LANES)
            for rb in range(n_blocks):
                r0 = rb * CHUNK
                strip = qkv_buf[r0:r0 + QKV_HIST + CHUNK, cs]
                acc = strip * cw_ref[0:1, cs]
                for j in range(1, GDN_CONV):
                    acc = pltpu.roll(acc, 1, axis=0) + strip * cw_ref[j:j + 1, cs]
                y = _silu(acc[QKV_HIST:QKV_HIST + CHUNK])
                if cb < 2 * GDN_HEADS:
                    scale = lax.rsqrt(jnp.sum(y * y, axis=-1, keepdims=True) + NORM_EPS)
                    if cb < GDN_HEADS:
                        scale = scale * (GDN_HEAD_DIM ** -0.5)
                    y = y * scale
                dst = (q_s, k_s, v_s)[cb // GDN_HEADS]
                hc = cb % GDN_HEADS
                dst[r0:r0 + CHUNK, hc * LANES:(hc + 1) * LANES] = y

    off31 = CONF_HIST - (CONF_KERNEL - 1)
    conv_cols = {}

    def conv31_block(rb, cb):
        r0 = rb * CHUNK
        cs = slice(cb * LANES, (cb + 1) * LANES)
        acc = None
        for b in range(SUBLANES):
            span = CHUNK + (SUBLANES if b else 0)
            z = None
            for a in range((CONF_KERNEL + off31 + SUBLANES - 1) // SUBLANES):
                j = SUBLANES * a + b - off31
                if 0 <= j < CONF_KERNEL:
                    term = y_buf[r0 + SUBLANES * a:r0 + SUBLANES * a + span, cs] * dww_ref[j:j + 1, cs]
                    z = term if z is None else z + term
            zs = z[b:b + CHUNK]
            acc = zs if acc is None else acc + zs
        conv_cols[rb, cb] = acc + dwb_ref[0:1, cs]
        if cb == W // LANES - 1:
            y = jnp.concatenate([conv_cols.pop((rb, i)) for i in range(W // LANES)], axis=1)
            mu = jnp.mean(y, axis=-1, keepdims=True)
            yc = y - mu
            var = jnp.mean(yc * yc, axis=-1, keepdims=True)
            yn = yc * lax.rsqrt(var + NORM_EPS) * lng_ref[0:1, :] + lnb_ref[0:1, :]
            yn_s[r0:r0 + CHUNK, :] = _silu(yn).astype(BF16)

    def project_bc(cols):
        p_s[:, cols] = _dot(h, wbc_ref[:, cols])

    n_qkv_groups = 3 * GDN_WIDTH // pw
    project_qkv(0)
    for gi in range(n_qkv_groups):
        if gi + 1 < n_qkv_groups:
            project_qkv(gi + 1)
        conv_group(gi)
    ba = _dot(h, wba_ref[...])
    row = lax.broadcasted_iota(jnp.int32, (tl, LANES), 0) + t * tl
    live = row >= pad
    beta_s[...] = jnp.where(live, _sigmoid(ba[:, 0:LANES]), 0.0)
    a_in = ba[:, LANES:2 * LANES] + dtb_ref[0:1, 0:LANES]
    softplus = jnp.maximum(a_in, 0.0) + jnp.log1p(jnp.exp(-jnp.abs(a_in)))
    g_s[...] = jnp.where(live, -jnp.exp(alog_ref[0:1, 0:LANES]) * softplus, 0.0)

    def project_za(n0):
        za_s[:, n0:n0 + pw] = _dot(h, wa_ref[:, 3 * GDN_WIDTH + n0:3 * GDN_WIDTH + n0 + pw])

    def glu():
        project_bc(UG)
        y_buf[CONF_HIST:CONF_HIST + tl, :] = p_s[:, U] * _sigmoid(p_s[:, UG])

    def project_cc():
        project_bc(CH)
        cc_buf[QKV_HIST:QKV_HIST + tl, :] = p_s[:, CC] * p_s[:, CH]

    def mixer_b_out():
        zb = _dot(yn_s[...], pww_ref[...]) + pwb_ref[0:1, :]
        mix_s[:, mix_b0:mix_c0] = (zb * _silu(p_s[:, ZB])).astype(BF16)

    def mixer_c_rows(rb):
        r0 = rb * CHUNK
        rows = slice(r0, r0 + CHUNK)
        for cb in range(W // LANES):
            cs = slice(cb * LANES, (cb + 1) * LANES)
            acc = None
            for j in range(SC_KERNEL):
                o = r0 + QKV_HIST - (SC_KERNEL - 1) + j
                term = cc_buf[o:o + CHUNK, cs] * scw_ref[j:j + 1, cs]
                acc = term if acc is None else acc + term
            pc = slice(3 * W + cb * LANES, 3 * W + (cb + 1) * LANES)
            pz = slice(6 * W + cb * LANES, 6 * W + (cb + 1) * LANES)
            yc3 = p_s[rows, pc] * acc * _silu(p_s[rows, pz])
            mix_s[rows, mix_c0 + cb * LANES:mix_c0 + (cb + 1) * LANES] = yc3.astype(BF16)

    out_cols = 2 * LANES

    def out_proj_bc(n0):
        out_ref[:, n0:n0 + out_cols] = x_ref[:, n0:n0 + out_cols] + _dot(
            mix_s[:, mix_b0:D_MIX], wo_ref[mix_b0:D_MIX, n0:n0 + out_cols])

    conv31 = [functools.partial(conv31_block, rb, cb)
              for rb in range(n_blocks) for cb in range(W // LANES)]
    projections = [functools.partial(project_za, n0) for n0 in range(0, GDN_WIDTH, pw)]
    projections += [functools.partial(project_bc, CC), project_cc, functools.partial(project_bc, CB),
                    functools.partial(project_bc, ZC), functools.partial(project_bc, ZB)]
    queue = [functools.partial(project_bc, U), glu]
    per_proj = -(-len(conv31) // len(projections))
    for i, proj in enumerate(projections):
        queue += conv31[i * per_proj:(i + 1) * per_proj] + [proj]
    queue += conv31[len(projections) * per_proj:]
    queue += [mixer_b_out] + [functools.partial(mixer_c_rows, rb) for rb in range(n_blocks)]
    queue += [functools.partial(out_proj_bc, n0) for n0 in range(0, D_MODEL, out_cols)]
    inv_steps = 6
    n_gaps = 4 * n_blocks + inv_steps + 1
    gap_no = [0]

    def fill_gap(last=False):
        i = gap_no[0]
        gap_no[0] += 1
        lo = min(i, n_gaps) * len(queue) // n_gaps
        hi = len(queue) if last else min(i + 1, n_gaps) * len(queue) // n_gaps
        for task in queue[lo:hi]:
            task()

    gw = HEAD_GROUP * LANES
    gc4 = HEAD_GROUP * CHUNK
    n_groups = GDN_HEADS // HEAD_GROUP
    ri = lax.broadcasted_iota(jnp.int32, (CHUNK, CHUNK), 0)
    ci = lax.broadcasted_iota(jnp.int32, (CHUNK, CHUNK), 1)
    tri_lo = (ri >= ci).astype(F32)
    ri2 = lax.broadcasted_iota(jnp.int32, (CHUNK, LANES), 0)
    ci2 = lax.broadcasted_iota(jnp.int32, (CHUNK, LANES), 1)
    tri_up2 = (ri2 <= (ci2 % CHUNK)).astype(F32)
    left = ci2 < CHUNK
    rt = lax.broadcasted_iota(jnp.int32, (CHUNK, gc4), 0)
    lt = lax.broadcasted_iota(jnp.int32, (CHUNK, gc4), 1) % CHUNK
    incl_t = rt >= lt
    strict_t = rt > lt
    eye_t = (rt == lt).astype(F32)
    nt_dims = (((1,), (1,)), ((), ()))
    tn_dims = (((0,), (0,)), ((), ()))
    chunk_rows = [slice(c * CHUNK, (c + 1) * CHUNK) for c in range(n_blocks)]
    heads = range(GDN_HEADS)
    hs = [slice(hd * LANES, (hd + 1) * LANES) for hd in heads]

    def head_blocks(ref, rows, g):
        return [ref[rows, hs[hd]] for hd in range(g * HEAD_GROUP, (g + 1) * HEAD_GROUP)]

    for c, rows in enumerate(chunk_rows):
        g_c = g_s[rows, :]
        gc = _dot_exact(tri_lo, g_c, ((1,), (0,)))
        gct = _dot_exact(g_c, tri_up2, ((0,), (0,)))
        gc_s[rows, :] = gc
        gct_s[c * SUBLANES:(c + 1) * SUBLANES, :] = gct[0:SUBLANES, :]
        eg = jnp.exp(gc)
        eg_s[rows, :] = eg
        edec = jnp.exp(gc[CHUNK - 1:CHUNK, :] - gc)
        beta_c = beta_s[rows, :]
        for hd in heads:
            cs = hs[hd]
            q = q_s[rows, cs]
            k = k_s[rows, cs]
            bcol = beta_c[:, hd:hd + 1]
            ecol = eg[:, hd:hd + 1]
            kb = k * bcol
            q16_s[rows, cs] = q.astype(BF16)
            k16_s[rows, cs] = k.astype(BF16)
            kb_s[rows, cs] = kb.astype(BF16)
            vb_s[rows, cs] = (v_s[rows, cs] * bcol).astype(BF16)
            kbg_s[rows, cs] = (kb * ecol).astype(BF16)
            qd_s[rows, cs] = (q * ecol).astype(BF16)
            kd_s[rows, cs] = (k * edec[:, hd:hd + 1]).astype(BF16)
        fill_gap()

    chains = [(c, g) for c in range(n_blocks) for g in range(n_groups)]
    tmat, xmat = {}, {}
    for c, rows in enumerate(chunk_rows):
        cexp = _bcast_heads_64(gc_s[rows, :], left)
        gt = gct_s[c * SUBLANES:(c + 1) * SUBLANES, :]
        rrow = jnp.concatenate(
            [jnp.where(left[0:1], gt[2 * p:2 * p + 1], gt[2 * p + 1:2 * p + 2])
             for p in range(GDN_HEADS // 2)], axis=1)
        dec_all = jnp.exp(cexp - rrow)
        for g in range(n_groups):
            cg = slice(g * gw, (g + 1) * gw)
            cl = slice(g * gc4, (g + 1) * gc4)
            lhs = jnp.concatenate([q16_s[rows, cg], kb_s[rows, cg]], axis=0)
            qkkk = lax.dot_general(lhs, _block_diag(head_blocks(k16_s, rows, g)), nt_dims,
                                   preferred_element_type=F32)
            dec = jnp.where(incl_t, dec_all[:, cl], 0.0)
            qk_s[rows, cl] = (qkkk[0:CHUNK] * dec).astype(BF16)
            nmat = jnp.where(strict_t, qkkk[CHUNK:2 * CHUNK] * dec, 0.0)
            tmat[c, g] = eye_t - nmat
            xmat[c, g] = nmat
        fill_gap()

    for key in chains:
        x16 = xmat[key].astype(BF16)
        xmat[key] = _dot(x16, _masked_block_diag(x16, m256_ref[...]))
    fill_gap()
    for step in range(1, inv_steps - 1):
        for key in chains:
            x16 = xmat[key].astype(BF16)
            r = _dot(jnp.concatenate([tmat[key].astype(BF16), x16], axis=0),
                     _masked_block_diag(x16, m256_ref[...]))
            tmat[key] = tmat[key] + r[0:CHUNK]
            xmat[key] = r[CHUNK:2 * CHUNK]
        fill_gap()
    for key in chains:
        x16 = xmat[key].astype(BF16)
        tmat[key] = tmat[key] + _dot(tmat[key].astype(BF16), _masked_block_diag(x16, m256_ref[...]))
    fill_gap()

    for c, g in chains:
        rows = chunk_rows[c]
        cg = slice(g * gw, (g + 1) * gw)
        t16 = tmat[c, g].astype(BF16)
        u_s[rows, cg] = _dot(t16, _block_diag(head_blocks(vb_s, rows, g)))
        w_s[rows, cg] = _dot(t16, _block_diag(head_blocks(kbg_s, rows, g))).astype(BF16)
    fill_gap()

    pairs = range(GDN_HEADS // 2)
    half = [slice(0, LANES), slice(LANES, 2 * LANES)]
    for c, rows in enumerate(chunk_rows):
        egl = eg_s[c * CHUNK + CHUNK - 1:(c + 1) * CHUNK, :]
        wq = [_dot(jnp.concatenate([w_s[rows, 2 * p * LANES:(2 * p + 2) * LANES],
                                    qd_s[rows, 2 * p * LANES:(2 * p + 2) * LANES]], axis=0),
                   _block_diag([st_ref[2 * p].astype(BF16), st_ref[2 * p + 1].astype(BF16)]))
              for p in pairs]
        fill_gap()
        vn16 = [(u_s[rows, hs[hd]] - wq[hd // 2][0:CHUNK, half[hd % 2]]).astype(BF16) for hd in heads]
        og = [_dot(qk_s[rows, p * LANES:(p + 1) * LANES], _block_diag([vn16[2 * p], vn16[2 * p + 1]]))
              for p in pairs]
        for hd in heads:
            st_ref[hd] = st_ref[hd] * egl[:, hd:hd + 1] + lax.dot_general(
                kd_s[rows, hs[hd]], vn16[hd], tn_dims, preferred_element_type=F32)
        fill_gap(last=(c == n_blocks - 1))
        for hd in heads:
            o = wq[hd // 2][CHUNK:2 * CHUNK, half[hd % 2]] + og[hd // 2][:, half[hd % 2]]
            o = o * lax.rsqrt(jnp.mean(o * o, axis=-1, keepdims=True) + NORM_EPS)
            o = o * gng_ref[0:1, 0:LANES] * _silu(za_s[rows, hs[hd]])
            mix_s[rows, hs[hd]] = o.astype(BF16)

    y = out_ref[...] + _dot(mix_s[:, 0:GDN_WIDTH], wo_ref[0:GDN_WIDTH, :])
    if final:
        y = y * lax.rsqrt(jnp.mean(y * y, axis=-1, keepdims=True) + NORM_EPS) * fg_ref[0:1, :]
    out_ref[...] = jnp.where(live[:, 0:1], y, 0.0)


def _const_spec(shape):
    nd = len(shape)
    return pl.BlockSpec(shape, lambda b, t: (0,) * nd, pipeline_mode=pl.Buffered(1))


def _row_spec(tl, width):
    return pl.BlockSpec((None, tl, width), lambda b, t: (b, t, 0))


def _pick_tile(lp):
    for tl in (192, 128, 64):
        if lp % tl == 0:
            return tl
    raise ValueError(f"padded sequence length {lp} is not a multiple of {CHUNK}")


def kernel(x, meta_tokens, norm_g, w_in, conv_qkv, a_log, dt_bias, gdn_norm_g, conf_dw_w, conf_dw_b, conf_ln_g, conf_ln_b, conf_pw_w, conf_pw_b, sc_conv_w, w_out, final_norm_g):
    bsz, seq, d = x.shape
    depth = w_in.shape[0]
    assert d == D_MODEL
    length = N_META + seq
    pad = (-length) % CHUNK
    lp = length + pad
    tl = _pick_tile(lp)
    grid = (bsz, lp // tl)

    meta = jnp.broadcast_to(meta_tokens.astype(x.dtype)[None], (bsz, N_META, d))
    hcur = jnp.concatenate([jnp.zeros((bsz, pad, d), x.dtype), meta, x], axis=1)

    o_z = 3 * GDN_WIDTH
    o_b = o_z + GDN_WIDTH
    o_a = o_b + GDN_HEADS
    o_u = o_a + GDN_HEADS
    w_a = w_in[:, :, 0:o_b].astype(BF16)
    w_ba = jnp.zeros((depth, d, 2 * LANES), F32)
    w_ba = w_ba.at[:, :, 0:GDN_HEADS].set(w_in[:, :, o_b:o_a])
    w_ba = w_ba.at[:, :, LANES:LANES + GDN_HEADS].set(w_in[:, :, o_a:o_u]).astype(BF16)
    w_bc = w_in[:, :, o_u:].astype(BF16)
    w_o = w_out.astype(BF16)
    pw_w = conf_pw_w.astype(BF16)
    def row2(a, width=None):
        a = a.astype(F32)
        width = a.shape[-1] if width is None else width
        return jnp.pad(a[:, None, :], ((0, 0), (0, PARAM_ROWS - 1), (0, width - a.shape[-1])))

    alog_p = row2(a_log, PARAM_LANES)
    dtb_p = row2(dt_bias, PARAM_LANES)
    gng_p = row2(gdn_norm_g, PARAM_LANES)
    scw_p = jnp.pad(sc_conv_w.astype(F32), ((0, 0), (0, PARAM_ROWS - SC_KERNEL), (0, 0)))
    blk = jnp.arange(HEAD_GROUP * CHUNK) // CHUNK
    m256 = (blk[:, None] == blk[None, :]).astype(BF16)

    wide = lambda: pltpu.VMEM((tl, GDN_WIDTH), F32)
    wide16 = lambda: pltpu.VMEM((tl, GDN_WIDTH), BF16)
    narrow = lambda: pltpu.VMEM((tl, LANES), F32)
    scratch = [
        wide16(),
        pltpu.VMEM((tl + QKV_HIST, 3 * GDN_WIDTH), F32),
        wide(), wide(), wide(), wide(),
        narrow(), narrow(),
        pltpu.VMEM((GDN_HEADS, GDN_HEAD_DIM, GDN_HEAD_DIM), F32),
        narrow(),
        pltpu.VMEM((-(-tl // CHUNK * SUBLANES // 32) * 32, LANES), F32),
        narrow(),
        wide16(), wide16(), wide16(), wide16(), wide16(), wide16(), wide16(),
        pltpu.VMEM((tl, GDN_WIDTH // 2), BF16),
        wide(),
        wide16(),
        pltpu.VMEM((tl, 7 * CONF_WIDTH), F32),
        pltpu.VMEM((tl + CONF_HIST, CONF_WIDTH), F32),
        pltpu.VMEM((tl + QKV_HIST, SC_WIDTH), F32),
        pltpu.VMEM((tl, CONF_WIDTH), BF16),
        pltpu.VMEM((tl, D_MIX), BF16),
    ]
    vec = lambda width: _const_spec((PARAM_ROWS, width))
    in_specs = [_row_spec(tl, d), vec(d), _const_spec((d, o_b)),
                _const_spec((d, 2 * LANES)), _const_spec((GDN_CONV, 3 * GDN_WIDTH)),
                vec(PARAM_LANES), vec(PARAM_LANES), vec(PARAM_LANES),
                _const_spec((HEAD_GROUP * CHUNK, HEAD_GROUP * CHUNK)),
                _const_spec((d, 7 * CONF_WIDTH)),
                _const_spec((CONF_KERNEL, CONF_WIDTH)), vec(CONF_WIDTH),
                vec(CONF_WIDTH), vec(CONF_WIDTH),
                _const_spec((CONF_WIDTH, CONF_WIDTH)), vec(CONF_WIDTH),
                vec(SC_WIDTH),
                _const_spec((D_MIX, d)), vec(d)]

    def layer_call(final):
        return pl.pallas_call(
            functools.partial(_layer_kernel, tl=tl, pad=pad, final=final),
            grid=grid, in_specs=in_specs, out_specs=_row_spec(tl, d),
            out_shape=jax.ShapeDtypeStruct((bsz, lp, d), F32),
            scratch_shapes=scratch,
            compiler_params=pltpu.CompilerParams(
                dimension_semantics=("arbitrary", "arbitrary"), vmem_limit_bytes=VMEM_LIMIT_BYTES),
            name="hybrid_layer")

    fg = row2(final_norm_g[None, :])[0]
    ng_p = row2(norm_g)
    for l in range(depth):
        hcur = layer_call(l == depth - 1)(
            hcur, ng_p[l], w_a[l], w_ba[l], conv_qkv[l].astype(F32),
            alog_p[l], dtb_p[l], gng_p[l], m256,
            w_bc[l], conf_dw_w[l].astype(F32), row2(conf_dw_b)[l], row2(conf_ln_g)[l],
            row2(conf_ln_b)[l], pw_w[l], row2(conf_pw_b)[l], scw_p[l],
            w_o[l], fg)
    return hcur[:, pad + N_META:]
```

```python
import functools
import math

import jax
import jax.numpy as jnp
from jax import lax
from jax.experimental import pallas as pl
from jax.experimental.pallas import tpu as pltpu

F32 = jnp.float32
BF16 = jnp.bfloat16

D_MODEL = 1024
N_META = 16
CHUNK = 64
NORM_EPS = 1e-6
GDN_HEADS = 8
GDN_HEAD_DIM = 128
GDN_WIDTH = GDN_HEADS * GDN_HEAD_DIM
GDN_CONV = 4
CONF_WIDTH = 512
CONF_KERNEL = 31
SC_WIDTH = 512
SC_KERNEL = 3
D_MIX = GDN_WIDTH + CONF_WIDTH + SC_WIDTH

HEAD_GROUP = 4
SUBLANES = 8
LANES = 128
QKV_HIST = 8
CONF_HIST = 32
VMEM_LIMIT_BYTES = 44 * 1024 * 1024
LOG2E = math.log2(math.e)


def _sigmoid(x):
    return 1.0 / (1.0 + jnp.exp2(x * (-LOG2E)))


def _silu(x):
    return x * _sigmoid(x)


def _dot(a, b):
    return jnp.dot(a, b, preferred_element_type=F32)


def _dot_exact(a, b, dims):
    return lax.dot_general(a, b, (dims, ((), ())), precision=lax.Precision.HIGHEST,
                           preferred_element_type=F32)


def _bcast_heads_64(a, left):
    r = a.shape[0]
    return jnp.concatenate(
        [jnp.where(left, jnp.broadcast_to(a[:, 2 * p:2 * p + 1], (r, LANES)),
                   jnp.broadcast_to(a[:, 2 * p + 1:2 * p + 2], (r, LANES)))
         for p in range(GDN_HEADS // 2)], axis=1)


def _masked_block_diag(x16, mask):
    return jnp.concatenate([x16] * HEAD_GROUP, axis=0) * mask


def _block_diag(blocks):
    zero = jnp.zeros_like(blocks[0])
    n = len(blocks)
    return jnp.concatenate(
        [jnp.concatenate([blocks[i] if i == j else zero for j in range(n)], axis=1)
         for i in range(n)], axis=0)


def _layer_kernel(x_ref, ng_ref, wa_ref, wba_ref, cw_ref, alog_ref, dtb_ref, gng_ref, m256_ref,
                  wbc_ref, dww_ref, dwb_ref, lng_ref, lnb_ref, pww_ref, pwb_ref, scw_ref,
                  wo_ref, fg_ref,
                  out_ref,
                  h_s, qkv_buf, q_s, k_s, v_s, za_s, beta_s, g_s, st_ref, gc_s, gct_s, eg_s,
                  q16_s, k16_s, kb_s, vb_s, kbg_s, qd_s, kd_s, qk_s, u_s, w_s,
                  p_s, y_buf, cc_buf, yn_s, mix_s,
                  *, tl, pad, final):
    t = pl.program_id(1)
    n_blocks = tl // CHUNK
    W = CONF_WIDTH
    U, UG, ZB, CB, CC, CH, ZC = (slice(i * W, (i + 1) * W) for i in range(7))
    mix_b0 = GDN_WIDTH
    mix_c0 = GDN_WIDTH + CONF_WIDTH

    @pl.when(t == 0)
    def _():
        qkv_buf[0:QKV_HIST, :] = jnp.zeros((QKV_HIST, 3 * GDN_WIDTH), F32)
        y_buf[0:CONF_HIST, :] = jnp.zeros((CONF_HIST, W), F32)
        cc_buf[0:QKV_HIST, :] = jnp.zeros((QKV_HIST, W), F32)
        st_ref[...] = jnp.zeros_like(st_ref)

    @pl.when(t > 0)
    def _():
        qkv_buf[0:QKV_HIST, :] = qkv_buf[tl:tl + QKV_HIST, :]
        y_buf[0:CONF_HIST, :] = y_buf[tl:tl + CONF_HIST, :]
        cc_buf[0:QKV_HIST, :] = cc_buf[tl:tl + QKV_HIST, :]

    for rb in range(n_blocks):
        rows = slice(rb * CHUNK, (rb + 1) * CHUNK)
        x = x_ref[rows, :]
        ms = jnp.mean(x * x, axis=-1, keepdims=True)
        h_s[rows, :] = (x * lax.rsqrt(ms + NORM_EPS) * ng_ref[...]).astype(BF16)
    h = h_s[...]


    pw = 4 * LANES

    def project_qkv(gi):
        qkv_buf[QKV_HIST:QKV_HIST + tl, gi * pw:(gi + 1) * pw] = _dot(h, wa_ref[:, gi * pw:(gi + 1) * pw])

    def conv_group(gi):
        for cb in range(gi * pw // LANES, (gi + 1) * pw // LANES):
            cs = slice(cb * LANES, (cb + 1) * LANES)
            for rb in range(n_blocks):
                r0 = rb * CHUNK
                strip = qkv_buf[r0:r0 + QKV_HIST + CHUNK, cs]
                acc = strip * cw_ref[0:1, cs]
                for j in range(1, GDN_CONV):
                    acc = pltpu.roll(acc, 1, axis=0) + strip * cw_ref[j:j + 1, cs]
                y = _silu(acc[QKV_HIST:QKV_HIST + CHUNK])
                if cb < 2 * GDN_HEADS:
                    scale = lax.rsqrt(jnp.sum(y * y, axis=-1, keepdims=True) + NORM_EPS)
                    if cb < GDN_HEADS:
                        scale = scale * (GDN_HEAD_DIM ** -0.5)
                    y = y * scale
                dst = (q_s, k_s, v_s)[cb // GDN_HEADS]
                hc = cb % GDN_HEADS
                dst[r0:r0 + CHUNK, hc * LANES:(hc + 1) * LANES] = y

    off31 = CONF_HIST - (CONF_KERNEL - 1)
    conv_cols = {}

    def conv31_block(rb, cb):
        r0 = rb * CHUNK
        cs = slice(cb * LANES, (cb + 1) * LANES)
        acc = None
        for b in range(SUBLANES):
            span = CHUNK + (SUBLANES if b else 0)
            z = None
            for a in range((CONF_KERNEL + off31 + SUBLANES - 1) // SUBLANES):
                j = SUBLANES * a + b - off31
                if 0 <= j < CONF_KERNEL:
                    term = y_buf[r0 + SUBLANES * a:r0 + SUBLANES * a + span, cs] * dww_ref[j:j + 1, cs]
                    z = term if z is None else z + term
            zs = z[b:b + CHUNK]
            acc = zs if acc is None else acc + zs
        conv_cols[rb, cb] = acc + dwb_ref[:, cs]
        if cb == W // LANES - 1:
            y = jnp.concatenate([conv_cols.pop((rb, i)) for i in range(W // LANES)], axis=1)
            mu = jnp.mean(y, axis=-1, keepdims=True)
            yc = y - mu
            var = jnp.mean(yc * yc, axis=-1, keepdims=True)
            yn = yc * lax.rsqrt(var + NORM_EPS) * lng_ref[...] + lnb_ref[...]
            yn_s[r0:r0 + CHUNK, :] = _silu(yn).astype(BF16)

    def project_bc(cols):
        p_s[:, cols] = _dot(h, wbc_ref[:, cols])

    n_qkv_groups = 3 * GDN_WIDTH // pw
    project_qkv(0)
    for gi in range(n_qkv_groups):
        if gi + 1 < n_qkv_groups:
            project_qkv(gi + 1)
        conv_group(gi)
    ba = _dot(h, wba_ref[...])
    row = lax.broadcasted_iota(jnp.int32, (tl, LANES), 0) + t * tl
    live = row >= pad
    beta_s[...] = jnp.where(live, _sigmoid(ba[:, 0:LANES]), 0.0)
    a_in = ba[:, LANES:2 * LANES] + dtb_ref[...]
    softplus = jnp.maximum(a_in, 0.0) + jnp.log1p(jnp.exp(-jnp.abs(a_in)))
    g_s[...] = jnp.where(live, -jnp.exp(alog_ref[...]) * softplus, 0.0)

    def project_za(n0):
        za_s[:, n0:n0 + pw] = _dot(h, wa_ref[:, 3 * GDN_WIDTH + n0:3 * GDN_WIDTH + n0 + pw])

    def glu():
        project_bc(UG)
        y_buf[CONF_HIST:CONF_HIST + tl, :] = p_s[:, U] * _sigmoid(p_s[:, UG])

    def project_cc():
        project_bc(CH)
        cc_buf[QKV_HIST:QKV_HIST + tl, :] = p_s[:, CC] * p_s[:, CH]

    def mixer_b_out():
        zb = _dot(yn_s[...], pww_ref[...]) + pwb_ref[...]
        mix_s[:, mix_b0:mix_c0] = (zb * _silu(p_s[:, ZB])).astype(BF16)

    def mixer_c_rows(rb):
        r0 = rb * CHUNK
        rows = slice(r0, r0 + CHUNK)
        for cb in range(W // LANES):
            cs = slice(cb * LANES, (cb + 1) * LANES)
            acc = None
            for j in range(SC_KERNEL):
                o = r0 + QKV_HIST - (SC_KERNEL - 1) + j
                term = cc_buf[o:o + CHUNK, cs] * scw_ref[j:j + 1, cs]
                acc = term if acc is None else acc + term
            pc = slice(3 * W + cb * LANES, 3 * W + (cb + 1) * LANES)
            pz = slice(6 * W + cb * LANES, 6 * W + (cb + 1) * LANES)
            yc3 = p_s[rows, pc] * acc * _silu(p_s[rows, pz])
            mix_s[rows, mix_c0 + cb * LANES:mix_c0 + (cb + 1) * LANES] = yc3.astype(BF16)

    out_cols = 2 * LANES

    def out_proj_bc(n0):
        out_ref[:, n0:n0 + out_cols] = x_ref[:, n0:n0 + out_cols] + _dot(
            mix_s[:, mix_b0:D_MIX], wo_ref[mix_b0:D_MIX, n0:n0 + out_cols])

    conv31 = [functools.partial(conv31_block, rb, cb)
              for rb in range(n_blocks) for cb in range(W // LANES)]
    projections = [functools.partial(project_za, n0) for n0 in range(0, GDN_WIDTH, pw)]
    projections += [functools.partial(project_bc, CC), project_cc, functools.partial(project_bc, CB),
                    functools.partial(project_bc, ZC), functools.partial(project_bc, ZB)]
    queue = [functools.partial(project_bc, U), glu]
    per_proj = -(-len(conv31) // len(projections))
    for i, proj in enumerate(projections):
        queue += conv31[i * per_proj:(i + 1) * per_proj] + [proj]
    queue += conv31[len(projections) * per_proj:]
    queue += [mixer_b_out] + [functools.partial(mixer_c_rows, rb) for rb in range(n_blocks)]
    queue += [functools.partial(out_proj_bc, n0) for n0 in range(0, D_MODEL, out_cols)]
    inv_steps = 6
    n_gaps = 4 * n_blocks + inv_steps + 1
    gap_no = [0]

    def fill_gap(last=False):
        i = gap_no[0]
        gap_no[0] += 1
        lo = min(i, n_gaps) * len(queue) // n_gaps
        hi = len(queue) if last else min(i + 1, n_gaps) * len(queue) // n_gaps
        for task in queue[lo:hi]:
            task()

    gw = HEAD_GROUP * LANES
    gc4 = HEAD_GROUP * CHUNK
    n_groups = GDN_HEADS // HEAD_GROUP
    ri = lax.broadcasted_iota(jnp.int32, (CHUNK, CHUNK), 0)
    ci = lax.broadcasted_iota(jnp.int32, (CHUNK, CHUNK), 1)
    tri_lo = (ri >= ci).astype(F32)
    ri2 = lax.broadcasted_iota(jnp.int32, (CHUNK, LANES), 0)
    ci2 = lax.broadcasted_iota(jnp.int32, (CHUNK, LANES), 1)
    tri_up2 = (ri2 <= (ci2 % CHUNK)).astype(F32)
    left = ci2 < CHUNK
    rt = lax.broadcasted_iota(jnp.int32, (CHUNK, gc4), 0)
    lt = lax.broadcasted_iota(jnp.int32, (CHUNK, gc4), 1) % CHUNK
    incl_t = rt >= lt
    strict_t = rt > lt
    eye_t = (rt == lt).astype(F32)
    nt_dims = (((1,), (1,)), ((), ()))
    tn_dims = (((0,), (0,)), ((), ()))
    chunk_rows = [slice(c * CHUNK, (c + 1) * CHUNK) for c in range(n_blocks)]
    heads = range(GDN_HEADS)
    hs = [slice(hd * LANES, (hd + 1) * LANES) for hd in heads]

    def head_blocks(ref, rows, g):
        return [ref[rows, hs[hd]] for hd in range(g * HEAD_GROUP, (g + 1) * HEAD_GROUP)]

    for c, rows in enumerate(chunk_rows):
        g_c = g_s[rows, :]
        gc = _dot_exact(tri_lo, g_c, ((1,), (0,)))
        gct = _dot_exact(g_c, tri_up2, ((0,), (0,)))
        gc_s[rows, :] = gc
        gct_s[c * SUBLANES:(c + 1) * SUBLANES, :] = gct[0:SUBLANES, :]
        eg = jnp.exp(gc)
        eg_s[rows, :] = eg
        edec = jnp.exp(gc[CHUNK - 1:CHUNK, :] - gc)
        beta_c = beta_s[rows, :]
        for hd in heads:
            cs = hs[hd]
            q = q_s[rows, cs]
            k = k_s[rows, cs]
            bcol = beta_c[:, hd:hd + 1]
            ecol = eg[:, hd:hd + 1]
            kb = k * bcol
            q16_s[rows, cs] = q.astype(BF16)
            k16_s[rows, cs] = k.astype(BF16)
            kb_s[rows, cs] = kb.astype(BF16)
            vb_s[rows, cs] = (v_s[rows, cs] * bcol).astype(BF16)
            kbg_s[rows, cs] = (kb * ecol).astype(BF16)
            qd_s[rows, cs] = (q * ecol).astype(BF16)
            kd_s[rows, cs] = (k * edec[:, hd:hd + 1]).astype(BF16)
        fill_gap()

    chains = [(c, g) for c in range(n_blocks) for g in range(n_groups)]
    tmat, xmat = {}, {}
    for c, rows in enumerate(chunk_rows):
        cexp = _bcast_heads_64(gc_s[rows, :], left)
        gt = gct_s[c * SUBLANES:(c + 1) * SUBLANES, :]
        rrow = jnp.concatenate(
            [jnp.where(left[0:1], gt[2 * p:2 * p + 1], gt[2 * p + 1:2 * p + 2])
             for p in range(GDN_HEADS // 2)], axis=1)
        dec_all = jnp.exp(cexp - rrow)
        for g in range(n_groups):
            cg = slice(g * gw, (g + 1) * gw)
            cl = slice(g * gc4, (g + 1) * gc4)
            lhs = jnp.concatenate([q16_s[rows, cg], kb_s[rows, cg]], axis=0)
            qkkk = lax.dot_general(lhs, _block_diag(head_blocks(k16_s, rows, g)), nt_dims,
                                   preferred_element_type=F32)
            dec = jnp.where(incl_t, dec_all[:, cl], 0.0)
            qk_s[rows, cl] = (qkkk[0:CHUNK] * dec).astype(BF16)
            nmat = jnp.where(strict_t, qkkk[CHUNK:2 * CHUNK] * dec, 0.0)
            tmat[c, g] = eye_t - nmat
            xmat[c, g] = nmat
        fill_gap()

    for key in chains:
        x16 = xmat[key].astype(BF16)
        xmat[key] = _dot(x16, _masked_block_diag(x16, m256_ref[...]))
    fill_gap()
    for step in range(1, inv_steps - 1):
        for key in chains:
            x16 = xmat[key].astype(BF16)
            r = _dot(jnp.concatenate([tmat[key].astype(BF16), x16], axis=0),
                     _masked_block_diag(x16, m256_ref[...]))
            tmat[key] = tmat[key] + r[0:CHUNK]
            xmat[key] = r[CHUNK:2 * CHUNK]
        fill_gap()
    for key in chains:
        x16 = xmat[key].astype(BF16)
        tmat[key] = tmat[key] + _dot(tmat[key].astype(BF16), _masked_block_diag(x16, m256_ref[...]))
    fill_gap()

    for c, g in chains:
        rows = chunk_rows[c]
        cg = slice(g * gw, (g + 1) * gw)
        t16 = tmat[c, g].astype(BF16)
        u_s[rows, cg] = _dot(t16, _block_diag(head_blocks(vb_s, rows, g)))
        w_s[rows, cg] = _dot(t16, _block_diag(head_blocks(kbg_s, rows, g))).astype(BF16)
    fill_gap()

    pairs = range(GDN_HEADS // 2)
    half = [slice(0, LANES), slice(LANES, 2 * LANES)]
    for c, rows in enumerate(chunk_rows):
        egl = eg_s[c * CHUNK + CHUNK - 1:(c + 1) * CHUNK, :]
        wq = [_dot(jnp.concatenate([w_s[rows, 2 * p * LANES:(2 * p + 2) * LANES],
                                    qd_s[rows, 2 * p * LANES:(2 * p + 2) * LANES]], axis=0),
                   _block_diag([st_ref[2 * p].astype(BF16), st_ref[2 * p + 1].astype(BF16)]))
              for p in pairs]
        fill_gap()
        vn16 = [(u_s[rows, hs[hd]] - wq[hd // 2][0:CHUNK, half[hd % 2]]).astype(BF16) for hd in heads]
        og = [_dot(qk_s[rows, p * LANES:(p + 1) * LANES], _block_diag([vn16[2 * p], vn16[2 * p + 1]]))
              for p in pairs]
        for hd in heads:
            st_ref[hd] = st_ref[hd] * egl[:, hd:hd + 1] + lax.dot_general(
                kd_s[rows, hs[hd]], vn16[hd], tn_dims, preferred_element_type=F32)
        fill_gap(last=(c == n_blocks - 1))
        for hd in heads:
            o = wq[hd // 2][CHUNK:2 * CHUNK, half[hd % 2]] + og[hd // 2][:, half[hd % 2]]
            o = o * lax.rsqrt(jnp.mean(o * o, axis=-1, keepdims=True) + NORM_EPS)
            o = o * gng_ref[...] * _silu(za_s[rows, hs[hd]])
            mix_s[rows, hs[hd]] = o.astype(BF16)

    y = out_ref[...] + _dot(mix_s[:, 0:GDN_WIDTH], wo_ref[0:GDN_WIDTH, :])
    if final:
        y = y * lax.rsqrt(jnp.mean(y * y, axis=-1, keepdims=True) + NORM_EPS) * fg_ref[...]
    out_ref[...] = jnp.where(live[:, 0:1], y, 0.0)


def _const_spec(shape):
    nd = len(shape)
    return pl.BlockSpec(shape, lambda b, t: (0,) * nd, pipeline_mode=pl.Buffered(1))


def _row_spec(tl, width):
    return pl.BlockSpec((None, tl, width), lambda b, t: (b, t, 0))


def _pick_tile(lp):
    for tl in (192, 128, 64):
        if lp % tl == 0:
            return tl
    raise ValueError(f"padded sequence length {lp} is not a multiple of {CHUNK}")


def kernel(x, meta_tokens, norm_g, w_in, conv_qkv, a_log, dt_bias, gdn_norm_g, conf_dw_w, conf_dw_b, conf_ln_g, conf_ln_b, conf_pw_w, conf_pw_b, sc_conv_w, w_out, final_norm_g):
    bsz, seq, d = x.shape
    depth = w_in.shape[0]
    assert d == D_MODEL
    length = N_META + seq
    pad = (-length) % CHUNK
    lp = length + pad
    tl = _pick_tile(lp)
    grid = (bsz, lp // tl)

    meta = jnp.broadcast_to(meta_tokens.astype(x.dtype)[None], (bsz, N_META, d))
    hcur = jnp.concatenate([jnp.zeros((bsz, pad, d), x.dtype), meta, x], axis=1)

    o_z = 3 * GDN_WIDTH
    o_b = o_z + GDN_WIDTH
    o_a = o_b + GDN_HEADS
    o_u = o_a + GDN_HEADS
    w_a = w_in[:, :, 0:o_b].astype(BF16)
    w_ba = jnp.zeros((depth, d, 2 * LANES), F32)
    w_ba = w_ba.at[:, :, 0:GDN_HEADS].set(w_in[:, :, o_b:o_a])
    w_ba = w_ba.at[:, :, LANES:LANES + GDN_HEADS].set(w_in[:, :, o_a:o_u]).astype(BF16)
    w_bc = w_in[:, :, o_u:].astype(BF16)
    w_o = w_out.astype(BF16)
    pw_w = conf_pw_w.astype(BF16)
    lane_pad = ((0, 0), (0, LANES - GDN_HEADS))
    alog_p = jnp.pad(a_log.astype(F32), lane_pad)[:, None, :]
    dtb_p = jnp.pad(dt_bias.astype(F32), lane_pad)[:, None, :]
    row2 = lambda a: a.astype(F32)[:, None, :]
    blk = jnp.arange(HEAD_GROUP * CHUNK) // CHUNK
    m256 = (blk[:, None] == blk[None, :]).astype(BF16)

    wide = lambda: pltpu.VMEM((tl, GDN_WIDTH), F32)
    wide16 = lambda: pltpu.VMEM((tl, GDN_WIDTH), BF16)
    narrow = lambda: pltpu.VMEM((tl, LANES), F32)
    scratch = [
        wide16(),
        pltpu.VMEM((tl + QKV_HIST, 3 * GDN_WIDTH), F32),
        wide(), wide(), wide(), wide(),
        narrow(), narrow(),
        pltpu.VMEM((GDN_HEADS, GDN_HEAD_DIM, GDN_HEAD_DIM), F32),
        narrow(),
        pltpu.VMEM((tl // CHUNK * SUBLANES, LANES), F32),
        narrow(),
        wide16(), wide16(), wide16(), wide16(), wide16(), wide16(), wide16(),
        pltpu.VMEM((tl, GDN_WIDTH // 2), BF16),
        wide(),
        wide16(),
        pltpu.VMEM((tl, 7 * CONF_WIDTH), F32),
        pltpu.VMEM((tl + CONF_HIST, CONF_WIDTH), F32),
        pltpu.VMEM((tl + QKV_HIST, SC_WIDTH), F32),
        pltpu.VMEM((tl, CONF_WIDTH), BF16),
        pltpu.VMEM((tl, D_MIX), BF16),
    ]
    in_specs = [_row_spec(tl, d), _const_spec((1, d)), _const_spec((d, o_b)),
                _const_spec((d, 2 * LANES)), _const_spec((GDN_CONV, 3 * GDN_WIDTH)),
                _const_spec((1, LANES)), _const_spec((1, LANES)), _const_spec((1, LANES)),
                _const_spec((HEAD_GROUP * CHUNK, HEAD_GROUP * CHUNK)),
                _const_spec((d, 7 * CONF_WIDTH)),
                _const_spec((CONF_KERNEL, CONF_WIDTH)), _const_spec((1, CONF_WIDTH)),
                _const_spec((1, CONF_WIDTH)), _const_spec((1, CONF_WIDTH)),
                _const_spec((CONF_WIDTH, CONF_WIDTH)), _const_spec((1, CONF_WIDTH)),
                _const_spec((SC_KERNEL, SC_WIDTH)),
                _const_spec((D_MIX, d)), _const_spec((1, d))]

    def layer_call(final):
        return pl.pallas_call(
            functools.partial(_layer_kernel, tl=tl, pad=pad, final=final),
            grid=grid, in_specs=in_specs, out_specs=_row_spec(tl, d),
            out_shape=jax.ShapeDtypeStruct((bsz, lp, d), F32),
            scratch_shapes=scratch,
            compiler_params=pltpu.CompilerParams(
                dimension_semantics=("arbitrary", "arbitrary"), vmem_limit_bytes=VMEM_LIMIT_BYTES),
            name="hybrid_layer")

    fg = final_norm_g.astype(F32)[None, :]
    for l in range(depth):
        hcur = layer_call(l == depth - 1)(
            hcur, norm_g[l].astype(F32)[None, :], w_a[l], w_ba[l], conv_qkv[l].astype(F32),
            alog_p[l], dtb_p[l], gdn_norm_g[l].astype(F32)[None, :], m256,
            w_bc[l], conf_dw_w[l].astype(F32), row2(conf_dw_b)[l], row2(conf_ln_g)[l],
            row2(conf_ln_b)[l], pw_w[l], row2(conf_pw_b)[l], sc_conv_w[l].astype(F32),
            w_o[l], fg)
    return hcur[:, pad + N_META:]
```
